```python
import math
import jax
import jax.numpy as jnp
from jax import lax
import numpy as np

D_MODEL = 2048
BATCH = 2
SEQ = 4096
DEPTH = 4
DEC_BATCH = 128
DEC_SEQ = 1
PAST_LEN = 8192
PAGE_SIZE = 128

N_MIXERS = 3
N_A_LAYERS = (DEPTH + 2) // 3
N_B_LAYERS = (DEPTH + 1) // 3
N_C_LAYERS = DEPTH // 3

ALPHA = (2.0 * DEPTH) ** 0.25
BETA = (8.0 * DEPTH) ** -0.25

Q_BLOCK = 128
NEG_INF = -1e30

SB_HEADS = 16
SB_HEAD_DIM = D_MODEL // SB_HEADS
SB_KV_HEADS = 4
SB_GROUP = SB_HEADS // SB_KV_HEADS
SB_SCALE = SB_HEAD_DIM ** -0.5

WIN_HEADS = 32
WIN_HEAD_DIM = D_MODEL // WIN_HEADS
WIN_KV_HEADS = 4
WIN_GROUP = WIN_HEADS // WIN_KV_HEADS
WINDOW = 128
WIN_SCALE = WIN_HEAD_DIM ** -0.5

MLA_HEADS = 16
MLA_NOPE = 128
MLA_ROPE = 64
MLA_V = 128
MLA_Q_LORA = D_MODEL // 4
MLA_KV_LORA = D_MODEL // 4
MLA_SCALE = (MLA_NOPE + MLA_ROPE) ** -0.5
ROPE_THETA = 10000.0

PEER_HEADS = 8
PEER_KEYS = 128
PEER_EXPERTS = PEER_KEYS * PEER_KEYS
PEER_TOPK = 16
PEER_QDIM = 256
PEER_HALF = PEER_QDIM // 2
PEER_BLOCK = 128

kernel_name = 'hybrid_sb_swa_mla_peer_step'


def layer_norm(x, g, b, eps=1e-5):
    xf = x.astype(jnp.float32)
    mu = jnp.mean(xf, axis=-1, keepdims=True)
    var = jnp.mean(jnp.square(xf - mu), axis=-1, keepdims=True)
    return ((xf - mu) * lax.rsqrt(var + eps) * g.astype(jnp.float32) + b.astype(jnp.float32)).astype(x.dtype)


def rms_norm(x, g, eps=1e-6):
    xf = x.astype(jnp.float32)
    return (xf * lax.rsqrt(jnp.mean(jnp.square(xf), axis=-1, keepdims=True) + eps) * g.astype(jnp.float32)).astype(x.dtype)


def rope(x, pos):
    inv_freq = ROPE_THETA ** (-jnp.arange(0, MLA_ROPE, 2, dtype=jnp.float32) / MLA_ROPE)
    ang = pos.astype(jnp.float32)[:, None] * inv_freq[None, :]
    ang = ang.reshape((ang.shape[0],) + (1,) * (x.ndim - 3) + (MLA_ROPE // 2,))
    cos, sin = jnp.cos(ang), jnp.sin(ang)
    x1, x2 = jnp.split(x.astype(jnp.float32), 2, axis=-1)
    return jnp.concatenate([x1 * cos - x2 * sin, x2 * cos + x1 * sin], axis=-1).astype(x.dtype)


def gather_pages(pool, layer, page_table):
    rows = pool[layer, page_table]
    return rows.reshape((rows.shape[0], rows.shape[1] * rows.shape[2]) + rows.shape[3:])


def map_query_blocks(fn, q_args, q_pos):
    n_blk = q_pos.shape[0] // Q_BLOCK

    def to_blocks(a):
        return jnp.moveaxis(a.reshape((a.shape[0], n_blk, Q_BLOCK) + a.shape[2:]), 1, 0)

    out = lax.map(lambda xs: fn(*xs[0], xs[1]),
                  (tuple(to_blocks(a) for a in q_args), q_pos.reshape(n_blk, Q_BLOCK)))
    out = jnp.moveaxis(out, 0, 1)
    return out.reshape((out.shape[0], n_blk * Q_BLOCK) + out.shape[3:])


def causal_softmax(s, q_pos, k_pos):
    return jax.nn.softmax(jnp.where(k_pos[None, :] <= q_pos[:, None], s, NEG_INF), axis=-1)


def sb_project(x, w_qkv):
    b, t, _ = x.shape
    q, k, v = jnp.split(x @ w_qkv, [SB_HEADS * SB_HEAD_DIM, (SB_HEADS + SB_KV_HEADS) * SB_HEAD_DIM], axis=-1)
    return (q.reshape(b, t, SB_KV_HEADS, SB_GROUP, SB_HEAD_DIM),
            k.reshape(b, t, SB_KV_HEADS, SB_HEAD_DIM),
            v.reshape(b, t, SB_KV_HEADS, SB_HEAD_DIM))


def sb_scores(q, k):
    return jnp.einsum('bqgrd,bkgd->bgrqk', q, k, preferred_element_type=jnp.float32) * SB_SCALE


def stick_breaking_weights(z, q_pos, k_pos):
    mask = k_pos[None, :] < q_pos[:, None]
    log_fail = jnp.where(mask, jax.nn.log_sigmoid(-z), 0.0)
    later = lax.cumsum(log_fail, axis=z.ndim - 1, reverse=True) - log_fail
    return jnp.where(mask, jnp.exp(jax.nn.log_sigmoid(z) + later), 0.0)


def sb_prompt(x, w_qkv, w_o):
    b, t, _ = x.shape
    q, k, v = sb_project(x, w_qkv)
    pos = jnp.arange(t)

    def block(qb, pb):
        a = stick_breaking_weights(sb_scores(qb, k), pb, pos)
        return jnp.einsum('bgrqk,bkgd->bqgrd', a.astype(v.dtype), v)

    o = map_query_blocks(block, (q,), pos)
    return o.reshape(b, t, -1) @ w_o, k, v


def sb_sample(x, w_qkv, w_o, cache_k, cache_v, layer, page_table):
    b, t, _ = x.shape
    q, k, v = sb_project(x, w_qkv)
    past_k = gather_pages(cache_k, layer, page_table)
    past_v = gather_pages(cache_v, layer, page_table)
    n_past = past_k.shape[1]
    z = jnp.concatenate([sb_scores(q, past_k), sb_scores(q, k)], axis=-1)
    a = stick_breaking_weights(z, n_past + jnp.arange(t), jnp.arange(n_past + t)).astype(v.dtype)
    o = (jnp.einsum('bgrqk,bkgd->bqgrd', a[..., :n_past], past_v)
         + jnp.einsum('bgrqk,bkgd->bqgrd', a[..., n_past:], v))
    return o.reshape(b, t, -1) @ w_o, k, v


def win_project(x, w_qkv):
    b, t, _ = x.shape
    q, k, v = jnp.split(x @ w_qkv, [WIN_HEADS * WIN_HEAD_DIM, (WIN_HEADS + WIN_KV_HEADS) * WIN_HEAD_DIM], axis=-1)
    return (q.reshape(b, t, WIN_KV_HEADS, WIN_GROUP, WIN_HEAD_DIM),
            k.reshape(b, t, WIN_KV_HEADS, WIN_HEAD_DIM),
            v.reshape(b, t, WIN_KV_HEADS, WIN_HEAD_DIM))


def alibi_slopes():
    return 2.0 ** (-8.0 * jnp.arange(1, WIN_HEADS + 1, dtype=jnp.float32) / WIN_HEADS)


def window_attention(q, k, v, q_pos, k_pos, sinks):
    s = jnp.einsum('...qgrd,...kgd->...grqk', q, k, preferred_element_type=jnp.float32) * WIN_SCALE
    dist = q_pos[..., :, None] - k_pos[..., None, :]
    valid = (dist >= 0) & (dist <= WINDOW) & (k_pos[..., None, :] >= 0)
    dist = dist[..., None, None, :, :].astype(jnp.float32)
    valid = valid[..., None, None, :, :]
    slopes = alibi_slopes().reshape(WIN_KV_HEADS, WIN_GROUP, 1, 1)
    s = jnp.where(valid, s - slopes * dist, NEG_INF)
    sink = jnp.broadcast_to(sinks.astype(jnp.float32).reshape(WIN_KV_HEADS, WIN_GROUP, 1, 1), s.shape[:-1] + (1,))
    p = jax.nn.softmax(jnp.concatenate([s, sink], axis=-1), axis=-1)[..., :-1]
    return jnp.einsum('...grqk,...kgd->...qgrd', p.astype(v.dtype), v)


def win_prompt(x, w_qkv, w_o, sinks):
    b, t, _ = x.shape
    q, k, v = win_project(x, w_qkv)
    nb = t // WINDOW

    def band(a):
        a = a.reshape((b, nb, WINDOW) + a.shape[2:])
        prev = jnp.pad(a, ((0, 0), (1, 0)) + ((0, 0),) * (a.ndim - 2))[:, :-1]
        return jnp.concatenate([prev, a], axis=2)

    q_pos = jnp.arange(t).reshape(nb, WINDOW)
    k_pos = (jnp.arange(nb)[:, None] - 1) * WINDOW + jnp.arange(2 * WINDOW)[None, :]
    o = window_attention(q.reshape((b, nb, WINDOW) + q.shape[2:]), band(k), band(v), q_pos, k_pos, sinks)
    keep = min(WINDOW, t)
    return o.reshape(b, t, -1) @ w_o, k[:, t - keep:], v[:, t - keep:]


def win_sample(x, w_qkv, w_o, sinks, buf_k, buf_v):
    b, t, _ = x.shape
    q, k, v = win_project(x, w_qkv)
    n_buf = buf_k.shape[1]
    k_all = jnp.concatenate([buf_k, k], axis=1)
    v_all = jnp.concatenate([buf_v, v], axis=1)
    k_pos = PAST_LEN - n_buf + jnp.arange(n_buf + t)
    q_pos = PAST_LEN + jnp.arange(t)
    o = window_attention(q, k_all, v_all, q_pos, k_pos, sinks)
    return o.reshape(b, t, -1) @ w_o, k_all[:, t:], v_all[:, t:]


def mla_project(x, pos, w_down, q_norm, kv_norm, w_uq):
    b, t, _ = x.shape
    cq, ckv, kr = jnp.split(x @ w_down, [MLA_Q_LORA, MLA_Q_LORA + MLA_KV_LORA], axis=-1)
    q = (rms_norm(cq, q_norm) @ w_uq).reshape(b, t, MLA_HEADS, MLA_NOPE + MLA_ROPE)
    q_nope, q_rope = jnp.split(q, [MLA_NOPE], axis=-1)
    return q_nope, rope(q_rope, pos), rms_norm(ckv, kv_norm), rope(kr, pos)


def mla_prompt(x, w_down, q_norm, kv_norm, w_uq, w_uk, w_uv, w_o):
    b, t, _ = x.shape
    pos = jnp.arange(t)
    q_nope, q_rope, ckv, kr = mla_project(x, pos, w_down, q_norm, kv_norm, w_uq)
    k_nope = jnp.einsum('btc,chn->bthn', ckv, w_uk)
    v = jnp.einsum('btc,chv->bthv', ckv, w_uv)

    def block(qn, qr, pb):
        s = (jnp.einsum('bqhn,bkhn->bhqk', qn, k_nope, preferred_element_type=jnp.float32)
             + jnp.einsum('bqhr,bkr->bhqk', qr, kr, preferred_element_type=jnp.float32)) * MLA_SCALE
        p = causal_softmax(s, pb, pos)
        return jnp.einsum('bhqk,bkhv->bqhv', p.astype(v.dtype), v)

    o = map_query_blocks(block, (q_nope, q_rope), pos)
    return o.reshape(b, t, -1) @ w_o, ckv, kr


def mla_sample(x, cache_ckv, cache_krope, layer, page_table, w_down, q_norm, kv_norm, w_uq, w_uk, w_uv, w_o):
    b, t, _ = x.shape
    pos = PAST_LEN + jnp.arange(t)
    q_nope, q_rope, ckv, kr = mla_project(x, pos, w_down, q_norm, kv_norm, w_uq)
    q_lat = jnp.einsum('bqhn,chn->bqhc', q_nope, w_uk)
    past_c = gather_pages(cache_ckv, layer, page_table)
    past_r = gather_pages(cache_krope, layer, page_table)
    n_past = past_c.shape[1]

    def part_scores(c, r):
        return (jnp.einsum('bqhc,bkc->bhqk', q_lat, c, preferred_element_type=jnp.float32)
                + jnp.einsum('bqhr,bkr->bhqk', q_rope, r, preferred_element_type=jnp.float32))

    s = jnp.concatenate([part_scores(past_c, past_r), part_scores(ckv, kr)], axis=-1) * MLA_SCALE
    p = causal_softmax(s, pos, jnp.arange(n_past + t)).astype(ckv.dtype)
    o_lat = (jnp.einsum('bhqk,bkc->bqhc', p[..., :n_past], past_c)
             + jnp.einsum('bhqk,bkc->bqhc', p[..., n_past:], ckv))
    o = jnp.einsum('bqhc,chv->bqhv', o_lat, w_uv)
    return o.reshape(b, t, -1) @ w_o, ckv, kr


def peer(x, w_q, subkeys, u, v):
    b, t, d = x.shape
    n = b * t
    xt = x.reshape(n, d)
    q = (xt @ w_q).reshape(n, PEER_HEADS, 2, PEER_HALF)
    s = jnp.einsum('nhpc,pkc->nhpk', q, subkeys, preferred_element_type=jnp.float32)
    s_top, i_top = lax.top_k(s, PEER_TOPK)
    cand = (s_top[:, :, 0, :, None] + s_top[:, :, 1, None, :]).reshape(n, PEER_HEADS, PEER_TOPK * PEER_TOPK)
    best, j = lax.top_k(cand, PEER_TOPK)
    idx_a = jnp.take_along_axis(i_top[:, :, 0], j // PEER_TOPK, axis=-1)
    idx_b = jnp.take_along_axis(i_top[:, :, 1], j % PEER_TOPK, axis=-1)
    experts = idx_a * PEER_KEYS + idx_b
    gates = jax.nn.softmax(best, axis=-1)

    pad = (-n) % PEER_BLOCK
    n_blk = (n + pad) // PEER_BLOCK

    def blocks(a):
        a = jnp.pad(a, [(0, pad)] + [(0, 0)] * (a.ndim - 1))
        return a.reshape((n_blk, PEER_BLOCK) + a.shape[1:])

    def expert_block(args):
        xb, eb, gb = args
        h = jax.nn.gelu(jnp.einsum('nd,nhkd->nhk', xb, u[eb], preferred_element_type=jnp.float32), approximate=False)
        return jnp.einsum('nhk,nhkd->nd', (gb * h).astype(xb.dtype), v[eb])

    y = lax.map(expert_block, (blocks(xt), blocks(experts), blocks(gates)))
    return y.reshape(n_blk * PEER_BLOCK, d)[:n].reshape(b, t, d)


def setup_inputs(seed: int = 0) -> dict:
    key = jax.random.key(seed)
    ks = jax.random.split(key, 40)
    d = D_MODEL
    n_pages = PAST_LEN // PAGE_SIZE
    n_used = DEC_BATCH * n_pages
    n_pool = n_used + n_used // 4
    win_buf = min(WINDOW, PAST_LEN)

    def nrm(i, shape, scale):
        return jax.random.normal(ks[i], shape, jnp.float32) * scale

    page_table = jax.random.permutation(ks[0], n_pool)[:n_used].reshape(DEC_BATCH, n_pages).astype(jnp.int32)
    sb_w_qkv = jnp.concatenate([
        nrm(1, (N_A_LAYERS, d, SB_HEADS * SB_HEAD_DIM), d ** -0.5),
        nrm(2, (N_A_LAYERS, d, SB_KV_HEADS * SB_HEAD_DIM), d ** -0.5),
        nrm(3, (N_A_LAYERS, d, SB_KV_HEADS * SB_HEAD_DIM), BETA * d ** -0.5)], axis=-1)
    win_w_qkv = jnp.concatenate([
        nrm(4, (N_B_LAYERS, d, WIN_HEADS * WIN_HEAD_DIM), d ** -0.5),
        nrm(5, (N_B_LAYERS, d, WIN_KV_HEADS * WIN_HEAD_DIM), d ** -0.5),
        nrm(6, (N_B_LAYERS, d, WIN_KV_HEADS * WIN_HEAD_DIM), BETA * d ** -0.5)], axis=-1)
    return {
        'x_prompt': nrm(7, (BATCH, SEQ, d), 1.0),
        'x_sample': nrm(8, (DEC_BATCH, DEC_SEQ, d), 1.0),
        'cache_sb_k': nrm(9, (N_A_LAYERS, n_pool, PAGE_SIZE, SB_KV_HEADS, SB_HEAD_DIM), 1.0),
        'cache_sb_v': nrm(10, (N_A_LAYERS, n_pool, PAGE_SIZE, SB_KV_HEADS, SB_HEAD_DIM), BETA),
        'cache_win_k': nrm(11, (N_B_LAYERS, DEC_BATCH, win_buf, WIN_KV_HEADS, WIN_HEAD_DIM), 1.0),
        'cache_win_v': nrm(12, (N_B_LAYERS, DEC_BATCH, win_buf, WIN_KV_HEADS, WIN_HEAD_DIM), BETA),
        'cache_mla_ckv': nrm(13, (N_C_LAYERS, n_pool, PAGE_SIZE, MLA_KV_LORA), 1.0),
        'cache_mla_krope': nrm(14, (N_C_LAYERS, n_pool, PAGE_SIZE, MLA_ROPE), 1.0),
        'page_table': page_table,
        'sb_w_qkv': sb_w_qkv,
        'sb_w_o': nrm(15, (N_A_LAYERS, SB_HEADS * SB_HEAD_DIM, d), BETA * (SB_HEADS * SB_HEAD_DIM) ** -0.5),
        'win_w_qkv': win_w_qkv,
        'win_w_o': nrm(16, (N_B_LAYERS, WIN_HEADS * WIN_HEAD_DIM, d), BETA * (WIN_HEADS * WIN_HEAD_DIM) ** -0.5),
        'win_sinks': nrm(17, (N_B_LAYERS, WIN_HEADS), 1.0),
        'mla_w_down': nrm(18, (N_C_LAYERS, d, MLA_Q_LORA + MLA_KV_LORA + MLA_ROPE), d ** -0.5),
        'mla_q_norm': 1.0 + nrm(19, (N_C_LAYERS, MLA_Q_LORA), 0.1),
        'mla_kv_norm': 1.0 + nrm(20, (N_C_LAYERS, MLA_KV_LORA), 0.1),
        'mla_w_uq': nrm(21, (N_C_LAYERS, MLA_Q_LORA, MLA_HEADS * (MLA_NOPE + MLA_ROPE)), MLA_Q_LORA ** -0.5),
        'mla_w_uk': nrm(22, (N_C_LAYERS, MLA_KV_LORA, MLA_HEADS, MLA_NOPE), MLA_KV_LORA ** -0.5),
        'mla_w_uv': nrm(23, (N_C_LAYERS, MLA_KV_LORA, MLA_HEADS, MLA_V), BETA * MLA_KV_LORA ** -0.5),
        'mla_w_o': nrm(24, (N_C_LAYERS, MLA_HEADS * MLA_V, d), BETA * (MLA_HEADS * MLA_V) ** -0.5),
        'peer_w_q': nrm(25, (DEPTH, d, PEER_HEADS * PEER_QDIM), d ** -0.5),
        'peer_subkeys': nrm(26, (DEPTH, 2, PEER_KEYS, PEER_HALF), PEER_HALF ** -0.5),
        'peer_u': nrm(27, (DEPTH, PEER_EXPERTS, d), d ** -0.5),
        'peer_v': nrm(28, (DEPTH, PEER_EXPERTS, d), BETA * PEER_HEADS ** -0.5),
        'ln_mix_g': 1.0 + nrm(29, (DEPTH, d), 0.1),
        'ln_mix_b': nrm(30, (DEPTH, d), 0.1),
        'ln_ffn_g': 1.0 + nrm(31, (DEPTH, d), 0.1),
        'ln_ffn_b': nrm(32, (DEPTH, d), 0.1),
    }


def reference(x_prompt, x_sample, cache_sb_k, cache_sb_v, cache_win_k, cache_win_v, cache_mla_ckv,
              cache_mla_krope, page_table, sb_w_qkv, sb_w_o, win_w_qkv, win_w_o, win_sinks, mla_w_down,
              mla_q_norm, mla_kv_norm, mla_w_uq, mla_w_uk, mla_w_uv, mla_w_o, peer_w_q, peer_subkeys,
              peer_u, peer_v, ln_mix_g, ln_mix_b, ln_ffn_g, ln_ffn_b):
    xp, xs = x_prompt, x_sample
    sb_kp, sb_vp, sb_ks, sb_vs = [], [], [], []
    win_kp, win_vp, win_ks, win_vs = [], [], [], []
    mla_cp, mla_rp, mla_cs, mla_rs = [], [], [], []
    for i in range(DEPTH):
        li = i // N_MIXERS
        kind = i % N_MIXERS
        if kind == 0:
            yp, kp, vp = sb_prompt(xp, sb_w_qkv[li], sb_w_o[li])
            ys, k_s, v_s = sb_sample(xs, sb_w_qkv[li], sb_w_o[li], cache_sb_k, cache_sb_v, li, page_table)
            sb_kp.append(kp); sb_vp.append(vp); sb_ks.append(k_s); sb_vs.append(v_s)
        elif kind == 1:
            yp, kp, vp = win_prompt(xp, win_w_qkv[li], win_w_o[li], win_sinks[li])
            ys, k_s, v_s = win_sample(xs, win_w_qkv[li], win_w_o[li], win_sinks[li], cache_win_k[li], cache_win_v[li])
            win_kp.append(kp); win_vp.append(vp); win_ks.append(k_s); win_vs.append(v_s)
        else:
            yp, cp, rp = mla_prompt(xp, mla_w_down[li], mla_q_norm[li], mla_kv_norm[li], mla_w_uq[li],
                                    mla_w_uk[li], mla_w_uv[li], mla_w_o[li])
            ys, c_s, r_s = mla_sample(xs, cache_mla_ckv, cache_mla_krope, li, page_table, mla_w_down[li],
                                      mla_q_norm[li], mla_kv_norm[li], mla_w_uq[li], mla_w_uk[li],
                                      mla_w_uv[li], mla_w_o[li])
            mla_cp.append(cp); mla_rp.append(rp); mla_cs.append(c_s); mla_rs.append(r_s)
        xp = layer_norm(ALPHA * xp + yp, ln_mix_g[i], ln_mix_b[i])
        xs = layer_norm(ALPHA * xs + ys, ln_mix_g[i], ln_mix_b[i])
        xp = layer_norm(ALPHA * xp + peer(xp, peer_w_q[i], peer_subkeys[i], peer_u[i], peer_v[i]), ln_ffn_g[i], ln_ffn_b[i])
        xs = layer_norm(ALPHA * xs + peer(xs, peer_w_q[i], peer_subkeys[i], peer_u[i], peer_v[i]), ln_ffn_g[i], ln_ffn_b[i])
    return (xp, xs,
            jnp.stack(sb_kp), jnp.stack(sb_vp), jnp.stack(sb_ks), jnp.stack(sb_vs),
            jnp.stack(win_kp), jnp.stack(win_vp), jnp.stack(win_ks), jnp.stack(win_vs),
            jnp.stack(mla_cp), jnp.stack(mla_rp), jnp.stack(mla_cs), jnp.stack(mla_rs))
```

```python
import functools
import math

import jax
import jax.numpy as jnp
from jax import lax
from jax.experimental import pallas as pl
from jax.experimental.pallas import tpu as pltpu

F32 = jnp.float32
BF16 = jnp.bfloat16

LANES = 128
VMEM_LIMIT_BYTES = 56 * 1024 * 1024

NEG_INF = -1e30
LN_EPS = 1e-5
RMS_EPS = 1e-6
ROPE_THETA = 10000.0
WINDOW = 128
PEER_TOPK = 16


def _params(*sem):
    return pltpu.CompilerParams(dimension_semantics=sem, vmem_limit_bytes=VMEM_LIMIT_BYTES)


def _dot(a, b):
    return jnp.dot(a, b, preferred_element_type=F32)


def _dot_nt(a, b):
    return lax.dot_general(a, b, (((1,), (1,)), ((), ())), preferred_element_type=F32)


def _softplus(z):
    return jnp.maximum(z, 0.0) + jnp.log1p(jnp.exp(-jnp.abs(z)))


def _gelu(x):
    return 0.5 * x * (1.0 + lax.erf(x * (2.0 ** -0.5)))


def _layer_norm(z, g, b):
    mu = jnp.mean(z, axis=-1, keepdims=True)
    zc = z - mu
    var = jnp.mean(zc * zc, axis=-1, keepdims=True)
    return zc * lax.rsqrt(var + LN_EPS) * g + b


def _rms_norm(z, g):
    return z * lax.rsqrt(jnp.mean(z * z, axis=-1, keepdims=True) + RMS_EPS) * g


def _col_tile(n, cap=1024):
    best = None
    for t in range(LANES, min(n, cap) + 1, LANES):
        if n % t == 0:
            best = t
    return best if best is not None else n


def _matmul_kernel(x_ref, w_ref, o_ref):
    o_ref[...] = _dot(x_ref[...].astype(BF16), w_ref[...]).astype(o_ref.dtype)


def _matmul(x, w, *, tm, out_dtype=F32, name):
    m, k = x.shape
    n = w.shape[1]
    tn = _col_tile(n)
    return pl.pallas_call(
        _matmul_kernel,
        grid=(m // tm, n // tn),
        in_specs=[pl.BlockSpec((tm, k), lambda i, j: (i, 0)),
                  pl.BlockSpec((k, tn), lambda i, j: (0, j))],
        out_specs=pl.BlockSpec((tm, tn), lambda i, j: (i, j)),
        out_shape=jax.ShapeDtypeStruct((m, n), out_dtype),
        compiler_params=_params("parallel", "arbitrary"),
        name=name,
    )(x, w)


def _matmul_ln_kernel(a_ref, w_ref, res_ref, g_ref, b_ref, o_ref, *, alpha):
    y = _dot(a_ref[...].astype(BF16), w_ref[...])
    o_ref[...] = _layer_norm(alpha * res_ref[...] + y, g_ref[...], b_ref[...])


def _matmul_ln(a, w, res, g, b, *, alpha, tm, name):
    m, k = a.shape
    n = w.shape[1]
    return pl.pallas_call(
        functools.partial(_matmul_ln_kernel, alpha=alpha),
        grid=(m // tm,),
        in_specs=[pl.BlockSpec((tm, k), lambda i: (i, 0)),
                  pl.BlockSpec((k, n), lambda i: (0, 0)),
                  pl.BlockSpec((tm, n), lambda i: (i, 0)),
                  pl.BlockSpec((1, n), lambda i: (0, 0)),
                  pl.BlockSpec((1, n), lambda i: (0, 0))],
        out_specs=pl.BlockSpec((tm, n), lambda i: (i, 0)),
        out_shape=jax.ShapeDtypeStruct((m, n), F32),
        compiler_params=_params("parallel"),
        name=name,
    )(a, w, res, g.reshape(1, n), b.reshape(1, n))


def _head_matmul_kernel(x_ref, w_ref, o_ref):
    o_ref[...] = _dot(x_ref[...].astype(BF16), w_ref[...]).astype(o_ref.dtype)


def _head_matmul(x, w, *, out_dtype, name):
    h, k, n = w.shape
    m = x.shape[0]
    return pl.pallas_call(
        _head_matmul_kernel,
        grid=(h,),
        in_specs=[pl.BlockSpec((m, k), lambda i: (0, i)),
                  pl.BlockSpec((None, k, n), lambda i: (i, 0, 0))],
        out_specs=pl.BlockSpec((m, n), lambda i: (0, i)),
        out_shape=jax.ShapeDtypeStruct((m, h * n), out_dtype),
        compiler_params=_params("parallel"),
        name=name,
    )(x, w)


def _tri_ge(n):
    ri = lax.broadcasted_iota(jnp.int32, (n, n), 0)
    ci = lax.broadcasted_iota(jnp.int32, (n, n), 1)
    return jnp.where(ri >= ci, 1.0, 0.0).astype(BF16)


def _sb_block(q, k, v, u, c, acc, scale, mask):
    z = _dot_nt(q, k) * scale
    lf = -_softplus(z)
    if mask is not None:
        lf = jnp.where(mask, lf, 0.0)
    hi = lf.astype(BF16)
    lo = (lf - hi.astype(F32)).astype(BF16)
    cum = _dot(hi, u) + _dot(lo, u)
    a = jnp.exp(z + cum + c)
    if mask is not None:
        a = jnp.where(mask, a, 0.0)
    acc = acc + _dot(a.astype(BF16), v)
    c = c + jnp.sum(lf, axis=-1, keepdims=True)
    return c, acc


def _sb_prompt_kernel(q_ref, k_ref, v_ref, o_ref, *, tq, group, d, scale):
    qi = pl.program_id(2)
    rows = group * tq
    q = jnp.concatenate([q_ref[:, r * d:(r + 1) * d] for r in range(group)], axis=0).astype(BF16)
    u = _tri_ge(tq)

    def block(kb, c, acc, mask):
        start = pl.multiple_of(kb * tq, tq)
        k = k_ref[pl.ds(start, tq), :].astype(BF16)
        v = v_ref[pl.ds(start, tq), :].astype(BF16)
        return _sb_block(q, k, v, u, c, acc, scale, mask)

    t_loc = lax.rem(lax.broadcasted_iota(jnp.int32, (rows, tq), 0), tq)
    s_loc = lax.broadcasted_iota(jnp.int32, (rows, tq), 1)
    c, acc = block(qi, jnp.zeros((rows, 1), F32), jnp.zeros((rows, d), F32), s_loc < t_loc)
    c, acc = lax.fori_loop(0, qi, lambda i, carry: block(qi - 1 - i, carry[0], carry[1], None), (c, acc))
    for r in range(group):
        o_ref[:, r * d:(r + 1) * d] = acc[r * tq:(r + 1) * tq].astype(o_ref.dtype)


def _sb_prompt_attn(qkv, *, batch, seq, heads, kv_heads, d, tq=128):
    group = heads // kv_heads
    nq = seq // tq
    kcol = heads
    vcol = heads + kv_heads
    return pl.pallas_call(
        functools.partial(_sb_prompt_kernel, tq=tq, group=group, d=d, scale=d ** -0.5),
        grid=(batch, kv_heads, nq),
        in_specs=[pl.BlockSpec((tq, group * d), lambda b, g, i: (b * nq + i, g)),
                  pl.BlockSpec((seq, d), lambda b, g, i: (b, kcol + g)),
                  pl.BlockSpec((seq, d), lambda b, g, i: (b, vcol + g))],
        out_specs=pl.BlockSpec((tq, group * d), lambda b, g, i: (b * nq + i, g)),
        out_shape=jax.ShapeDtypeStruct((batch * seq, heads * d), BF16),
        compiler_params=_params("parallel", "parallel", "arbitrary"),
        name="sb_prompt_attn",
    )(qkv, qkv, qkv)


def _block_diag_queries(q, group, d, width):
    heads = q.shape[0]
    qt = jnp.concatenate([q] * (width // d), axis=1)
    rg = lax.broadcasted_iota(jnp.int32, (heads, width), 0) // group
    lg = lax.broadcasted_iota(jnp.int32, (heads, width), 1) // d
    return jnp.where(rg == lg, qt, 0.0)


def _select_own_block(acc, group, d):
    heads, width = acc.shape
    rg = lax.broadcasted_iota(jnp.int32, (heads, d), 0) // group
    out = jnp.zeros((heads, d), F32)
    for g in range(width // d):
        out = out + jnp.where(rg == g, acc[:, g * d:(g + 1) * d], 0.0)
    return out


def _sb_sample_kernel(pt_ref, q_ref, kn_ref, vn_ref, *rest, pp, group, d, scale, n_past):
    k_refs = rest[:pp]
    v_refs = rest[pp:2 * pp]
    o_ref, c_scr, acc_scr = rest[2 * pp:]
    s = pl.program_id(1)
    heads = q_ref.shape[0]
    page, width = k_refs[0].shape
    qbd = _block_diag_queries(q_ref[...], group, d, width)
    qbd_bf = qbd.astype(BF16)
    u = _tri_ge(page)

    @pl.when(s == 0)
    def _():
        z = jnp.sum(qbd * kn_ref[...], axis=-1, keepdims=True) * scale
        q_pos = n_past + lax.broadcasted_iota(jnp.int32, (heads, 1), 1)
        k_pos = n_past + lax.broadcasted_iota(jnp.int32, (heads, 1), 1)
        vis = k_pos < q_pos
        lf = jnp.where(vis, -_softplus(z), 0.0)
        a = jnp.where(vis, jnp.exp(z + lf), 0.0)
        c_scr[...] = lf
        acc_scr[...] = a * jnp.broadcast_to(vn_ref[...], (heads, width))

    c = c_scr[...]
    acc = acc_scr[...]
    for i in range(pp):
        k = k_refs[i][...].astype(BF16)
        v = v_refs[i][...].astype(BF16)
        c, acc = _sb_block(qbd_bf, k, v, u, c, acc, scale, None)
    c_scr[...] = c
    acc_scr[...] = acc

    @pl.when(s == pl.num_programs(1) - 1)
    def _():
        o_ref[...] = _select_own_block(acc, group, d).astype(o_ref.dtype)


def _sb_sample_attn(q, k_new, v_new, cache_k, cache_v, layer, page_table, *, group, d, pp=4):
    bsz, heads, _ = q.shape
    _, _, page, width = cache_k.shape
    n_pages = page_table.shape[1]
    steps = n_pages // pp

    def page_spec(i):
        return pl.BlockSpec((None, None, page, width),
                            lambda b, s, pt: (layer, pt[b, n_pages - 1 - (s * pp + i)], 0, 0))

    grid_spec = pltpu.PrefetchScalarGridSpec(
        num_scalar_prefetch=1,
        grid=(bsz, steps),
        in_specs=[pl.BlockSpec((None, heads, d), lambda b, s, pt: (b, 0, 0)),
                  pl.BlockSpec((None, 1, width), lambda b, s, pt: (b, 0, 0)),
                  pl.BlockSpec((None, 1, width), lambda b, s, pt: (b, 0, 0))]
                 + [page_spec(i) for i in range(pp)] + [page_spec(i) for i in range(pp)],
        out_specs=pl.BlockSpec((None, heads, d), lambda b, s, pt: (b, 0, 0)),
        scratch_shapes=[pltpu.VMEM((heads, 1), F32), pltpu.VMEM((heads, width), F32)],
    )
    return pl.pallas_call(
        functools.partial(_sb_sample_kernel, pp=pp, group=group, d=d, scale=d ** -0.5,
                          n_past=n_pages * page),
        grid_spec=grid_spec,
        out_shape=jax.ShapeDtypeStruct((bsz, heads, d), BF16),
        compiler_params=_params("parallel", "arbitrary"),
        name="sb_sample_attn",
    )(page_table, q, k_new, v_new, *([cache_k] * pp), *([cache_v] * pp))


def _win_softmax(s, sink):
    m = jnp.maximum(jnp.max(s, axis=-1, keepdims=True), sink)
    p = jnp.exp(s - m)
    denom = jnp.sum(p, axis=-1, keepdims=True) + jnp.exp(sink - m)
    return p / denom


def _win_prompt_kernel(sink_ref, q_ref, kp_ref, kc_ref, vp_ref, vc_ref, o_ref, *, kv_heads, group, d, scale):
    nb = pl.program_id(1)
    tq = q_ref.shape[0]
    t_loc = lax.broadcasted_iota(jnp.int32, (tq, 2 * tq), 0)
    j_loc = lax.broadcasted_iota(jnp.int32, (tq, 2 * tq), 1)
    dist = t_loc + tq - j_loc
    k_pos = (nb - 1) * tq + j_loc
    valid = (dist >= 0) & (dist <= WINDOW) & (k_pos >= 0)
    dist_f = dist.astype(F32)
    for g in range(kv_heads):
        kb = jnp.concatenate([kp_ref[:, g * d:(g + 1) * d], kc_ref[:, g * d:(g + 1) * d]], axis=0).astype(BF16)
        vb = jnp.concatenate([vp_ref[:, g * d:(g + 1) * d], vc_ref[:, g * d:(g + 1) * d]], axis=0).astype(BF16)
        outs = []
        for r in range(group):
            h = g * group + r
            slope = 2.0 ** (-8.0 * (h + 1) / (kv_heads * group))
            qh = q_ref[:, h * d:(h + 1) * d].astype(BF16)
            s = _dot_nt(qh, kb) * scale
            s = jnp.where(valid, s - slope * dist_f, NEG_INF)
            p = _win_softmax(s, sink_ref[h])
            outs.append(_dot(p.astype(BF16), vb))
        for r in range(0, group, 2):
            h = g * group + r
            o_ref[:, h * d:(h + 2) * d] = jnp.concatenate(outs[r:r + 2], axis=-1).astype(o_ref.dtype)


def _win_prompt_attn(qkv, sinks, *, batch, seq, heads, kv_heads, d):
    tq = WINDOW
    nb = seq // tq
    group = heads // kv_heads
    qw = heads * d
    kvw = kv_heads * d
    kcol = qw // kvw
    vcol = kcol + 1
    grid_spec = pltpu.PrefetchScalarGridSpec(
        num_scalar_prefetch=0,
        grid=(batch, nb),
        in_specs=[pl.BlockSpec(memory_space=pltpu.SMEM),
                  pl.BlockSpec((tq, qw), lambda b, i: (b * nb + i, 0)),
                  pl.BlockSpec((tq, kvw), lambda b, i: (b * nb + jnp.maximum(i - 1, 0), kcol)),
                  pl.BlockSpec((tq, kvw), lambda b, i: (b * nb + i, kcol)),
                  pl.BlockSpec((tq, kvw), lambda b, i: (b * nb + jnp.maximum(i - 1, 0), vcol)),
                  pl.BlockSpec((tq, kvw), lambda b, i: (b * nb + i, vcol))],
        out_specs=pl.BlockSpec((tq, qw), lambda b, i: (b * nb + i, 0)),
    )
    return pl.pallas_call(
        functools.partial(_win_prompt_kernel, kv_heads=kv_heads, group=group, d=d, scale=d ** -0.5),
        grid_spec=grid_spec,
        out_shape=jax.ShapeDtypeStruct((batch * seq, qw), BF16),
        compiler_params=_params("parallel", "arbitrary"),
        name="win_prompt_attn",
    )(sinks, qkv, qkv, qkv, qkv, qkv)


def _win_sample_kernel(sink_ref, q_ref, kn_ref, vn_ref, kb_ref, vb_ref, o_ref, ko_ref, vo_ref,
                       *, group, d, scale, past_len):
    heads = q_ref.shape[0]
    n_buf, width = kb_ref.shape
    qbd = _block_diag_queries(q_ref[...], group, d, width)
    kbuf = kb_ref[...]
    vbuf = vb_ref[...]
    kn = kn_ref[...]
    vn = vn_ref[...]
    slope = jnp.exp2(-8.0 * (lax.broadcasted_iota(jnp.int32, (heads, 1), 0) + 1).astype(F32) / heads)
    q_pos = past_len
    j = lax.broadcasted_iota(jnp.int32, (heads, n_buf), 1)
    k_pos = past_len - n_buf + j
    dist = q_pos - k_pos
    valid = (dist >= 0) & (dist <= WINDOW) & (k_pos >= 0)
    s_buf = _dot_nt(qbd.astype(BF16), kbuf.astype(BF16)) * scale
    s_buf = jnp.where(valid, s_buf - slope * dist.astype(F32), NEG_INF)
    s_new = jnp.sum(qbd * kn, axis=-1, keepdims=True) * scale
    sink = sink_ref[...]
    m = jnp.maximum(jnp.maximum(jnp.max(s_buf, axis=-1, keepdims=True), s_new), sink)
    p_buf = jnp.exp(s_buf - m)
    p_new = jnp.exp(s_new - m)
    denom = jnp.sum(p_buf, axis=-1, keepdims=True) + p_new + jnp.exp(sink - m)
    acc = _dot((p_buf / denom).astype(BF16), vbuf.astype(BF16)) + (p_new / denom) * vn
    o_ref[...] = _select_own_block(acc, group, d).astype(o_ref.dtype)
    last = lax.broadcasted_iota(jnp.int32, (n_buf, width), 0) == n_buf - 1
    ko_ref[...] = jnp.where(last, kn, pltpu.roll(kbuf, n_buf - 1, 0))
    vo_ref[...] = jnp.where(last, vn, pltpu.roll(vbuf, n_buf - 1, 0))


def _win_sample_attn(q, k_new, v_new, buf_k, buf_v, sinks, *, group, d, past_len):
    bsz, heads, _ = q.shape
    _, n_buf, width = buf_k.shape
    row = lambda b: (b, 0, 0)
    return pl.pallas_call(
        functools.partial(_win_sample_kernel, group=group, d=d, scale=d ** -0.5, past_len=past_len),
        grid=(bsz,),
        in_specs=[pl.BlockSpec((heads, 1), lambda b: (0, 0)),
                  pl.BlockSpec((None, heads, d), row),
                  pl.BlockSpec((None, 1, width), row),
                  pl.BlockSpec((None, 1, width), row),
                  pl.BlockSpec((None, n_buf, width), row),
                  pl.BlockSpec((None, n_buf, width), row)],
        out_specs=[pl.BlockSpec((None, heads, d), row),
                   pl.BlockSpec((None, n_buf, width), row),
                   pl.BlockSpec((None, n_buf, width), row)],
        out_shape=[jax.ShapeDtypeStruct((bsz, heads, d), BF16),
                   jax.ShapeDtypeStruct((bsz, n_buf, width), F32),
                   jax.ShapeDtypeStruct((bsz, n_buf, width), F32)],
        compiler_params=_params("parallel"),
        name="win_sample_attn",
    )(sinks.reshape(heads, 1), q, k_new, v_new, buf_k, buf_v)


def _rope_tables(pos, rope_dim):
    inv_freq = ROPE_THETA ** (-jnp.arange(0, rope_dim, 2, dtype=F32) / rope_dim)
    ang = pos.astype(F32)[:, None] * inv_freq[None, :]
    cos, sin = jnp.cos(ang), jnp.sin(ang)
    reps = LANES // rope_dim
    return (jnp.concatenate([cos, cos] * reps, axis=-1),
            jnp.concatenate([-sin, sin] * reps, axis=-1))


def _rope_lanes(x, cos_t, sin_t, half):
    lane = lax.broadcasted_iota(jnp.int32, x.shape, 1)
    partner = jnp.where(lax.rem(lane, 2 * half) < half,
                        pltpu.roll(x, LANES - half, 1), pltpu.roll(x, half, 1))
    return x * cos_t + partner * sin_t


def _mla_down_kernel(x_ref, w_ref, qn_ref, kvn_ref, cos_ref, sin_ref, cq_ref, ckv_ref, kr_ref, kr2_ref,
                     *, q_lora, kv_lora, rope_dim):
    acc = _dot(x_ref[...].astype(BF16), w_ref[...])
    cq_ref[...] = _rms_norm(acc[:, :q_lora], qn_ref[...]).astype(cq_ref.dtype)
    ckv_ref[...] = _rms_norm(acc[:, q_lora:q_lora + kv_lora], kvn_ref[...])
    kr = acc[:, q_lora + kv_lora:]
    kr = _rope_lanes(kr, cos_ref[...], sin_ref[...], rope_dim // 2)
    lane = lax.broadcasted_iota(jnp.int32, kr.shape, 1)
    kr = jnp.where(lane < rope_dim, kr, 0.0)
    kr_ref[...] = kr[:, :rope_dim]
    kr2_ref[...] = (kr + pltpu.roll(kr, rope_dim, 1)).astype(kr2_ref.dtype)


def _mla_down(x, w_pad, q_norm, kv_norm, cos_t, sin_t, *, q_lora, kv_lora, rope_dim, tm, name):
    m, k = x.shape
    n = w_pad.shape[1]
    rowb = lambda w: pl.BlockSpec((tm, w), lambda i: (i, 0))
    full = lambda r, c: pl.BlockSpec((r, c), lambda i: (0, 0))
    return pl.pallas_call(
        functools.partial(_mla_down_kernel, q_lora=q_lora, kv_lora=kv_lora, rope_dim=rope_dim),
        grid=(m // tm,),
        in_specs=[rowb(k), full(k, n), full(1, q_lora), full(1, kv_lora), rowb(LANES), rowb(LANES)],
        out_specs=[rowb(q_lora), rowb(kv_lora), rowb(rope_dim), rowb(LANES)],
        out_shape=[jax.ShapeDtypeStruct((m, q_lora), BF16),
                   jax.ShapeDtypeStruct((m, kv_lora), F32),
                   jax.ShapeDtypeStruct((m, rope_dim), F32),
                   jax.ShapeDtypeStruct((m, LANES), BF16)],
        compiler_params=_params("parallel"),
        name=name,
    )(x, w_pad, q_norm.reshape(1, -1), kv_norm.reshape(1, -1), cos_t, sin_t)


def _mla_uq_kernel(cq_ref, w_ref, cos_ref, sin_ref, qn_ref, qr_ref, *, nope_w, rope_dim):
    acc = _dot(cq_ref[...], w_ref[...])
    qn_ref[...] = acc[:, :nope_w].astype(qn_ref.dtype)
    cos_t = cos_ref[...]
    sin_t = sin_ref[...]
    for c in range(qr_ref.shape[1] // LANES):
        x = acc[:, nope_w + c * LANES:nope_w + (c + 1) * LANES]
        qr_ref[:, c * LANES:(c + 1) * LANES] = _rope_lanes(x, cos_t, sin_t, rope_dim // 2).astype(qr_ref.dtype)


def _mla_uq(cq, w_perm, cos_t, sin_t, *, nope_w, rope_w, rope_dim, tm, name):
    m, k = cq.shape
    n = w_perm.shape[1]
    rowb = lambda w: pl.BlockSpec((tm, w), lambda i: (i, 0))
    return pl.pallas_call(
        functools.partial(_mla_uq_kernel, nope_w=nope_w, rope_dim=rope_dim),
        grid=(m // tm,),
        in_specs=[rowb(k), pl.BlockSpec((k, n), lambda i: (0, 0)), rowb(LANES), rowb(LANES)],
        out_specs=[rowb(nope_w), rowb(rope_w)],
        out_shape=[jax.ShapeDtypeStruct((m, nope_w), BF16), jax.ShapeDtypeStruct((m, rope_w), BF16)],
        compiler_params=_params("parallel"),
        name=name,
    )(cq, w_perm, cos_t, sin_t)


def _mla_prompt_kernel(qn_ref, qr_ref, kn_ref, kr_ref, v_ref, o_ref, *, tq, d, rope_dim, scale):
    qi = pl.program_id(2)
    lane = lax.broadcasted_iota(jnp.int32, (tq, LANES), 1)
    t_loc = lax.broadcasted_iota(jnp.int32, (tq, tq), 0)
    s_loc = lax.broadcasted_iota(jnp.int32, (tq, tq), 1)
    causal = s_loc <= t_loc
    qr_all = qr_ref[...]
    for hh in range(LANES // rope_dim):
        qn = qn_ref[:, hh * d:(hh + 1) * d]
        qr = jnp.where((lane >= hh * rope_dim) & (lane < (hh + 1) * rope_dim), qr_all, 0.0).astype(BF16)

        def block(kb, m, l, acc, mask, hh=hh, qn=qn, qr=qr):
            start = pl.multiple_of(kb * tq, tq)
            kn = kn_ref[pl.ds(start, tq), hh * d:(hh + 1) * d]
            kr = kr_ref[pl.ds(start, tq), :]
            v = v_ref[pl.ds(start, tq), hh * d:(hh + 1) * d]
            s = (_dot_nt(qn, kn) + _dot_nt(qr, kr)) * scale
            if mask is not None:
                s = jnp.where(mask, s, NEG_INF)
            m_new = jnp.maximum(m, jnp.max(s, axis=-1, keepdims=True))
            alpha = jnp.exp(m - m_new)
            p = jnp.exp(s - m_new)
            l = alpha * l + jnp.sum(p, axis=-1, keepdims=True)
            acc = alpha * acc + _dot(p.astype(BF16), v)
            return m_new, l, acc

        init = (jnp.full((tq, 1), NEG_INF, F32), jnp.zeros((tq, 1), F32), jnp.zeros((tq, d), F32))
        m, l, acc = block(qi, *init, causal)
        m, l, acc = lax.fori_loop(0, qi, lambda i, c: block(i, c[0], c[1], c[2], None), (m, l, acc))
        o_ref[:, hh * d:(hh + 1) * d] = (acc / l).astype(o_ref.dtype)


def _mla_prompt_attn(q_nope, q_rope, kv, kr2, *, batch, seq, heads, d, rope_dim, tq=256):
    nq = seq // tq
    hp = LANES // rope_dim
    return pl.pallas_call(
        functools.partial(_mla_prompt_kernel, tq=tq, d=d, rope_dim=rope_dim, scale=(d + rope_dim) ** -0.5),
        grid=(batch, heads // hp, nq),
        in_specs=[pl.BlockSpec((tq, hp * d), lambda b, h, i: (b * nq + i, h)),
                  pl.BlockSpec((tq, LANES), lambda b, h, i: (b * nq + i, h)),
                  pl.BlockSpec((seq, hp * d), lambda b, h, i: (b, h)),
                  pl.BlockSpec((seq, LANES), lambda b, h, i: (b, 0)),
                  pl.BlockSpec((seq, hp * d), lambda b, h, i: (b, heads // hp + h))],
        out_specs=pl.BlockSpec((tq, hp * d), lambda b, h, i: (b * nq + i, h)),
        out_shape=jax.ShapeDtypeStruct((batch * seq, heads * d), BF16),
        compiler_params=_params("parallel", "parallel", "arbitrary"),
        name="mla_prompt_attn",
    )(q_nope, q_rope, kv, kr2, kv)


def _mla_sample_kernel(pt_ref, ql_ref, qr_ref, cn_ref, rn_ref, *rest, pp, scale, n_past):
    c_refs = rest[:pp]
    r_refs = rest[pp:2 * pp]
    o_ref, m_scr, l_scr, acc_scr = rest[2 * pp:]
    s_id = pl.program_id(1)
    heads, width = ql_ref.shape
    ql = ql_ref[...]
    qr = qr_ref[...]

    @pl.when(s_id == 0)
    def _():
        cn = cn_ref[...]
        s = (jnp.sum(ql * cn, axis=-1, keepdims=True) + jnp.sum(qr * rn_ref[...], axis=-1, keepdims=True)) * scale
        q_pos = n_past + lax.broadcasted_iota(jnp.int32, (heads, 1), 1)
        k_pos = n_past + lax.broadcasted_iota(jnp.int32, (heads, 1), 1)
        vis = k_pos <= q_pos
        m = jnp.where(vis, s, NEG_INF)
        p = jnp.where(vis, jnp.exp(s - m), 0.0)
        m_scr[...] = m
        l_scr[...] = p
        acc_scr[...] = p * jnp.broadcast_to(cn, (heads, width))

    ql_bf = ql.astype(BF16)
    qr_bf = qr.astype(BF16)
    m = m_scr[...]
    l = l_scr[...]
    acc = acc_scr[...]
    for i in range(pp):
        c = c_refs[i][...].astype(BF16)
        r = r_refs[i][...].astype(BF16)
        s = (_dot_nt(ql_bf, c) + _dot_nt(qr_bf, r)) * scale
        m_new = jnp.maximum(m, jnp.max(s, axis=-1, keepdims=True))
        alpha = jnp.exp(m - m_new)
        p = jnp.exp(s - m_new)
        l = alpha * l + jnp.sum(p, axis=-1, keepdims=True)
        acc = alpha * acc + _dot(p.astype(BF16), c)
        m = m_new
    m_scr[...] = m
    l_scr[...] = l
    acc_scr[...] = acc

    @pl.when(s_id == pl.num_programs(1) - 1)
    def _():
        o_ref[...] = (acc / l).astype(o_ref.dtype)


def _mla_sample_attn(q_lat, q_rope, c_new, r_new, cache_c, cache_r, layer, page_table, *, scale, pp=4):
    bsz, heads, width = q_lat.shape
    rope_dim = q_rope.shape[2]
    page = cache_c.shape[2]
    n_pages = page_table.shape[1]
    steps = n_pages // pp

    def page_spec(w, i):
        return pl.BlockSpec((None, None, page, w), lambda b, s, pt: (layer, pt[b, s * pp + i], 0, 0))

    row = lambda b, s, pt: (b, 0, 0)
    grid_spec = pltpu.PrefetchScalarGridSpec(
        num_scalar_prefetch=1,
        grid=(bsz, steps),
        in_specs=[pl.BlockSpec((None, heads, width), row),
                  pl.BlockSpec((None, heads, rope_dim), row),
                  pl.BlockSpec((None, 1, width), row),
                  pl.BlockSpec((None, 1, rope_dim), row)]
                 + [page_spec(width, i) for i in range(pp)] + [page_spec(rope_dim, i) for i in range(pp)],
        out_specs=pl.BlockSpec((None, heads, width), row),
        scratch_shapes=[pltpu.VMEM((heads, 1), F32), pltpu.VMEM((heads, 1), F32),
                        pltpu.VMEM((heads, width), F32)],
    )
    return pl.pallas_call(
        functools.partial(_mla_sample_kernel, pp=pp, scale=scale, n_past=n_pages * page),
        grid_spec=grid_spec,
        out_shape=jax.ShapeDtypeStruct((bsz, heads, width), BF16),
        compiler_params=_params("parallel", "arbitrary"),
        name="mla_sample_attn",
    )(page_table, q_lat, q_rope, c_new, r_new, *([cache_c] * pp), *([cache_r] * pp))


def _top_values(x, k):
    tops = []
    for _ in range(k):
        m = jnp.max(x, axis=0, keepdims=True)
        tops.append(m)
        x = jnp.where(x == m, -jnp.inf, x)
    return jnp.concatenate(tops, axis=0)


def _peer_route_kernel(x_ref, wq_ref, sk_ref, sa_ref, sb_ref, ea_ref, eb_ref, tau_ref, *, heads, half):
    q = _dot(x_ref[...].astype(BF16), wq_ref[...]).astype(BF16)
    taus = []
    for h in range(heads):
        qa = q[:, (2 * h) * half:(2 * h + 1) * half]
        qb = q[:, (2 * h + 1) * half:(2 * h + 2) * half]
        sa = _dot_nt(sk_ref[0], qa)
        sb = _dot_nt(sk_ref[1], qb)
        ta = _top_values(sa, PEER_TOPK)
        tb = _top_values(sb, PEER_TOPK)
        cand = jnp.concatenate([ta[i:i + 1] + tb for i in range(PEER_TOPK)], axis=0)
        tau = _top_values(cand, PEER_TOPK)[PEER_TOPK - 1:PEER_TOPK]
        best = ta[0:1] + tb[0:1]
        z = jnp.sum(jnp.where(cand >= tau, jnp.exp(cand - best), 0.0), axis=0, keepdims=True)
        sa_ref[h] = sa
        sb_ref[h] = sb
        ea_ref[h] = jnp.exp(sa - ta[0:1]) / z
        eb_ref[h] = jnp.exp(sb - tb[0:1])
        taus.append(tau)
    tau_ref[...] = jnp.concatenate(taus, axis=0)


def _peer_route(x, w_q, subkeys, *, heads, tn, name):
    n, dm = x.shape
    _, keys, half = subkeys.shape
    big = pl.BlockSpec((heads, keys, tn), lambda i: (0, 0, i))
    big_shape = jax.ShapeDtypeStruct((heads, keys, n), F32)
    return pl.pallas_call(
        functools.partial(_peer_route_kernel, heads=heads, half=half),
        grid=(n // tn,),
        in_specs=[pl.BlockSpec((tn, dm), lambda i: (i, 0)),
                  pl.BlockSpec(w_q.shape, lambda i: (0, 0)),
                  pl.BlockSpec(subkeys.shape, lambda i: (0, 0, 0))],
        out_specs=[big, big, big, big, pl.BlockSpec((heads, tn), lambda i: (0, i))],
        out_shape=[big_shape, big_shape, big_shape, big_shape, jax.ShapeDtypeStruct((heads, n), F32)],
        compiler_params=_params("parallel"),
        name=name,
    )(x, w_q, subkeys)


def _peer_dense_kernel(x_ref, sa_ref, sb_ref, ea_ref, eb_ref, tau_ref, u_ref, vt_ref, g_ref, b_ref, o_ref,
                       xt_scr, yt_scr, pt_scr, *, heads, keys, alpha):
    j = pl.program_id(1)
    tn = x_ref.shape[0]
    ce = u_ref.shape[0]

    @pl.when(j == 0)
    def _():
        xt_scr[...] = x_ref[...].T.astype(BF16)
        yt_scr[...] = jnp.zeros_like(yt_scr)

    ht = _dot(u_ref[...], xt_scr[...])
    for ai in range(ce // keys):
        for tt in range(tn // LANES):
            tok = slice(tt * LANES, (tt + 1) * LANES)
            gate = jnp.zeros((keys, LANES), F32)
            for h in range(heads):
                sa_row = sa_ref[h, ai:ai + 1, tok]
                ea_row = ea_ref[h, ai:ai + 1, tok]
                chosen = (sa_row + sb_ref[h, :, tok]) >= tau_ref[h:h + 1, tok]
                gate = gate + jnp.where(chosen, ea_row * eb_ref[h, :, tok], 0.0)
            hv = ht[ai * keys:(ai + 1) * keys, tok]
            pt_scr[ai * keys:(ai + 1) * keys, tok] = (gate * _gelu(hv)).astype(BF16)
    yt_scr[...] += _dot(vt_ref[...], pt_scr[...])

    @pl.when(j == pl.num_programs(1) - 1)
    def _():
        o_ref[...] = _layer_norm(alpha * x_ref[...] + yt_scr[...].T, g_ref[...], b_ref[...])


def _peer_dense(x, route, u_bf, vt_bf, g, b, *, alpha, tn, ce, name):
    sa, sb, ea, eb, tau = route
    n, dm = x.shape
    heads, keys, _ = sa.shape
    n_exp = u_bf.shape[0]
    once = pl.Buffered(1)
    by_b = pl.BlockSpec((heads, keys, tn), lambda i, j: (0, 0, i), pipeline_mode=once)
    by_a = pl.BlockSpec((heads, ce // keys, tn), lambda i, j: (0, j, i))
    return pl.pallas_call(
        functools.partial(_peer_dense_kernel, heads=heads, keys=keys, alpha=alpha),
        grid=(n // tn, n_exp // ce),
        in_specs=[pl.BlockSpec((tn, dm), lambda i, j: (i, 0), pipeline_mode=once),
                  by_a, by_b, by_a, by_b,
                  pl.BlockSpec((heads, tn), lambda i, j: (0, i)),
                  pl.BlockSpec((ce, dm), lambda i, j: (j, 0)),
                  pl.BlockSpec((dm, ce), lambda i, j: (0, j)),
                  pl.BlockSpec((1, dm), lambda i, j: (0, 0)),
                  pl.BlockSpec((1, dm), lambda i, j: (0, 0))],
        out_specs=pl.BlockSpec((tn, dm), lambda i, j: (i, 0)),
        out_shape=jax.ShapeDtypeStruct((n, dm), F32),
        scratch_shapes=[pltpu.VMEM((dm, tn), BF16), pltpu.VMEM((dm, tn), F32), pltpu.VMEM((ce, tn), BF16)],
        compiler_params=_params("parallel", "arbitrary"),
        name=name,
    )(x, sa, sb, ea, eb, tau, u_bf, vt_bf, g.reshape(1, dm), b.reshape(1, dm))


def _peer_layer(x, w_q, subkeys, u_bf, vt_bf, g, b, *, alpha, heads, tn_route, tn_dense, ce, tag):
    route = _peer_route(x, w_q, subkeys, heads=heads, tn=tn_route, name="peer_route_" + tag)
    return _peer_dense(x, route, u_bf, vt_bf, g, b, alpha=alpha, tn=tn_dense, ce=ce, name="peer_dense_" + tag)


def kernel(x_prompt, x_sample, cache_sb_k, cache_sb_v, cache_win_k, cache_win_v, cache_mla_ckv, cache_mla_krope, page_table, sb_w_qkv, sb_w_o, win_w_qkv, win_w_o, win_sinks, mla_w_down, mla_q_norm, mla_kv_norm, mla_w_uq, mla_w_uk, mla_w_uv, mla_w_o, peer_w_q, peer_subkeys, peer_u, peer_v, ln_mix_g, ln_mix_b, ln_ffn_g, ln_ffn_b):
    batch, seq, dm = x_prompt.shape
    bsz, dec_seq, _ = x_sample.shape
    assert dec_seq == 1
    depth = peer_w_q.shape[0]
    alpha = (2.0 * depth) ** 0.25
    page = cache_sb_k.shape[2]
    n_pages = page_table.shape[1]
    past_len = n_pages * page

    sb_kv, sb_d = cache_sb_k.shape[3], cache_sb_k.shape[4]
    sb_heads = dm // sb_d
    win_kv, win_d = cache_win_k.shape[3], cache_win_k.shape[4]
    win_heads = win_sinks.shape[1]
    q_lora = mla_q_norm.shape[1]
    kv_lora = mla_kv_norm.shape[1]
    rope_dim = cache_mla_krope.shape[3]
    mla_heads, mla_nope = mla_w_uk.shape[2], mla_w_uk.shape[3]
    mla_v = mla_w_uv.shape[3]
    peer_heads = peer_w_q.shape[2] // (2 * peer_subkeys.shape[3])

    n_p = batch * seq
    tm_p = 512 if n_p % 512 == 0 else n_p
    tm_ln = 256 if n_p % 256 == 0 else n_p
    tn_route_p = 256 if n_p % 256 == 0 else n_p
    tn_dense_p = 512 if n_p % 512 == 0 else n_p

    xp = x_prompt.reshape(n_p, dm)
    xs = x_sample.reshape(bsz, dm)

    cache_sb_k2 = cache_sb_k.reshape(cache_sb_k.shape[:3] + (sb_kv * sb_d,))
    cache_sb_v2 = cache_sb_v.reshape(cache_sb_v.shape[:3] + (sb_kv * sb_d,))

    cos_p, sin_p = _rope_tables(jnp.arange(seq), rope_dim)
    cos_p = jnp.tile(cos_p, (batch, 1))
    sin_p = jnp.tile(sin_p, (batch, 1))
    cos_s, sin_s = _rope_tables(jnp.full((bsz,), past_len), rope_dim)

    outs = {k: [] for k in ("sb_kp", "sb_vp", "sb_ks", "sb_vs", "win_kp", "win_vp", "win_ks", "win_vs",
                            "mla_cp", "mla_rp", "mla_cs", "mla_rs")}

    for i in range(depth):
        li, kind = divmod(i, 3)
        g_mix, b_mix = ln_mix_g[i], ln_mix_b[i]
        if kind == 0:
            w_qkv = sb_w_qkv[li].astype(BF16)
            w_o = sb_w_o[li].astype(BF16)
            qw, kvw = sb_heads * sb_d, sb_kv * sb_d
            qkv_p = _matmul(xp, w_qkv, tm=tm_p, name="sb_qkv_p")
            qkv_s = _matmul(xs, w_qkv, tm=bsz, name="sb_qkv_s")
            o_p = _sb_prompt_attn(qkv_p, batch=batch, seq=seq, heads=sb_heads, kv_heads=sb_kv, d=sb_d)
            k_s = qkv_s[:, qw:qw + kvw]
            v_s = qkv_s[:, qw + kvw:]
            o_s = _sb_sample_attn(qkv_s[:, :qw].reshape(bsz, sb_heads, sb_d), k_s.reshape(bsz, 1, kvw),
                                  v_s.reshape(bsz, 1, kvw), cache_sb_k2, cache_sb_v2, li, page_table,
                                  group=sb_heads // sb_kv, d=sb_d)
            xp = _matmul_ln(o_p, w_o, xp, g_mix, b_mix, alpha=alpha, tm=tm_ln, name="sb_out_p")
            xs = _matmul_ln(o_s.reshape(bsz, qw), w_o, xs, g_mix, b_mix, alpha=alpha, tm=bsz, name="sb_out_s")
            outs["sb_kp"].append(qkv_p[:, qw:qw + kvw].reshape(batch, seq, sb_kv, sb_d))
            outs["sb_vp"].append(qkv_p[:, qw + kvw:].reshape(batch, seq, sb_kv, sb_d))
            outs["sb_ks"].append(k_s.reshape(bsz, 1, sb_kv, sb_d))
            outs["sb_vs"].append(v_s.reshape(bsz, 1, sb_kv, sb_d))
        elif kind == 1:
            w_qkv = win_w_qkv[li].astype(BF16)
            w_o = win_w_o[li].astype(BF16)
            qw, kvw = win_heads * win_d, win_kv * win_d
            qkv_p = _matmul(xp, w_qkv, tm=tm_p, name="win_qkv_p")
            qkv_s = _matmul(xs, w_qkv, tm=bsz, name="win_qkv_s")
            o_p = _win_prompt_attn(qkv_p, win_sinks[li], batch=batch, seq=seq, heads=win_heads,
                                   kv_heads=win_kv, d=win_d)
            n_buf = cache_win_k.shape[2]
            o_s, nk_s, nv_s = _win_sample_attn(
                qkv_s[:, :qw].reshape(bsz, win_heads, win_d), qkv_s[:, qw:qw + kvw].reshape(bsz, 1, kvw),
                qkv_s[:, qw + kvw:].reshape(bsz, 1, kvw), cache_win_k[li].reshape(bsz, n_buf, kvw),
                cache_win_v[li].reshape(bsz, n_buf, kvw), win_sinks[li],
                group=win_heads // win_kv, d=win_d, past_len=past_len)
            xp = _matmul_ln(o_p, w_o, xp, g_mix, b_mix, alpha=alpha, tm=tm_ln, name="win_out_p")
            xs = _matmul_ln(o_s.reshape(bsz, qw), w_o, xs, g_mix, b_mix, alpha=alpha, tm=bsz, name="win_out_s")
            keep = min(WINDOW, seq)
            kp = qkv_p[:, qw:qw + kvw].reshape(batch, seq, win_kv, win_d)
            vp = qkv_p[:, qw + kvw:].reshape(batch, seq, win_kv, win_d)
            outs["win_kp"].append(kp[:, seq - keep:])
            outs["win_vp"].append(vp[:, seq - keep:])
            outs["win_ks"].append(nk_s.reshape(bsz, n_buf, win_kv, win_d))
            outs["win_vs"].append(nv_s.reshape(bsz, n_buf, win_kv, win_d))
        else:
            down_w = q_lora + kv_lora + rope_dim
            w_down = jnp.pad(mla_w_down[li], ((0, 0), (0, q_lora + kv_lora + LANES - down_w))).astype(BF16)
            w_uq = mla_w_uq[li].reshape(q_lora, mla_heads, mla_nope + rope_dim)
            w_uq = jnp.concatenate([w_uq[:, :, :mla_nope].reshape(q_lora, -1),
                                    w_uq[:, :, mla_nope:].reshape(q_lora, -1)], axis=1).astype(BF16)
            nope_w, rope_w = mla_heads * mla_nope, mla_heads * rope_dim
            w_kv = jnp.concatenate([mla_w_uk[li].reshape(kv_lora, -1), mla_w_uv[li].reshape(kv_lora, -1)],
                                   axis=1).astype(BF16)
            w_uk_t = jnp.transpose(mla_w_uk[li], (1, 2, 0)).astype(BF16)
            w_uv_h = jnp.transpose(mla_w_uv[li], (1, 0, 2)).astype(BF16)
            w_o = mla_w_o[li].astype(BF16)
            mla_kw = dict(q_lora=q_lora, kv_lora=kv_lora, rope_dim=rope_dim)
            uq_kw = dict(nope_w=nope_w, rope_w=rope_w, rope_dim=rope_dim)
            scale = (mla_nope + rope_dim) ** -0.5

            cq_p, ckv_p, kr_p, kr2_p = _mla_down(xp, w_down, mla_q_norm[li], mla_kv_norm[li], cos_p, sin_p,
                                                 tm=tm_p, name="mla_down_p", **mla_kw)
            qn_p, qr_p = _mla_uq(cq_p, w_uq, cos_p, sin_p, tm=tm_p, name="mla_uq_p", **uq_kw)
            kv_p = _matmul(ckv_p, w_kv, tm=tm_p, out_dtype=BF16, name="mla_kv_p")
            o_p = _mla_prompt_attn(qn_p, qr_p, kv_p, kr2_p, batch=batch, seq=seq, heads=mla_heads,
                                   d=mla_nope, rope_dim=rope_dim, tq=256 if seq % 256 == 0 else seq)
            xp = _matmul_ln(o_p, w_o, xp, g_mix, b_mix, alpha=alpha, tm=tm_ln, name="mla_out_p")

            cq_s, ckv_s, kr_s, _ = _mla_down(xs, w_down, mla_q_norm[li], mla_kv_norm[li], cos_s, sin_s,
                                             tm=bsz, name="mla_down_s", **mla_kw)
            qn_s, qr_s = _mla_uq(cq_s, w_uq, cos_s, sin_s, tm=bsz, name="mla_uq_s", **uq_kw)
            q_lat = _head_matmul(qn_s, w_uk_t, out_dtype=F32, name="mla_qlat_s")
            o_lat = _mla_sample_attn(q_lat.reshape(bsz, mla_heads, kv_lora),
                                     qr_s.astype(F32).reshape(bsz, mla_heads, rope_dim),
                                     ckv_s.reshape(bsz, 1, kv_lora), kr_s.reshape(bsz, 1, rope_dim),
                                     cache_mla_ckv, cache_mla_krope, li, page_table, scale=scale)
            o_s = _head_matmul(o_lat.reshape(bsz, mla_heads * kv_lora), w_uv_h, out_dtype=BF16, name="mla_ov_s")
            xs = _matmul_ln(o_s, w_o, xs, g_mix, b_mix, alpha=alpha, tm=bsz, name="mla_out_s")
            outs["mla_cp"].append(ckv_p.reshape(batch, seq, kv_lora))
            outs["mla_rp"].append(kr_p.reshape(batch, seq, rope_dim))
            outs["mla_cs"].append(ckv_s.reshape(bsz, 1, kv_lora))
            outs["mla_rs"].append(kr_s.reshape(bsz, 1, rope_dim))

        w_q = peer_w_q[i].astype(BF16)
        subkeys = peer_subkeys[i].astype(BF16)
        u_bf = peer_u[i].astype(BF16)
        vt_bf = peer_v[i].astype(BF16).T
        peer_kw = dict(alpha=alpha, heads=peer_heads, ce=8 * peer_subkeys.shape[2])
        xp = _peer_layer(xp, w_q, subkeys, u_bf, vt_bf, ln_ffn_g[i], ln_ffn_b[i], tn_route=tn_route_p,
                         tn_dense=tn_dense_p, tag="p", **peer_kw)
        xs = _peer_layer(xs, w_q, subkeys, u_bf, vt_bf, ln_ffn_g[i], ln_ffn_b[i], tn_route=bsz,
                         tn_dense=bsz, tag="s", **peer_kw)

    stack = lambda k: jnp.stack(outs[k])
    return (xp.reshape(batch, seq, dm), xs.reshape(bsz, 1, dm),
            stack("sb_kp"), stack("sb_vp"), stack("sb_ks"), stack("sb_vs"),
            stack("win_kp"), stack("win_vp"), stack("win_ks"), stack("win_vs"),
            stack("mla_cp"), stack("mla_rp"), stack("mla_cs"), stack("mla_rs"))
```

```python
import functools
import math

import jax
import jax.numpy as jnp
from jax import lax
from jax.experimental import pallas as pl
from jax.experimental.pallas import tpu as pltpu

F32 = jnp.float32
BF16 = jnp.bfloat16

LANES = 128
VMEM_LIMIT_BYTES = 56 * 1024 * 1024

NEG_INF = -1e30
LN_EPS = 1e-5
RMS_EPS = 1e-6
ROPE_THETA = 10000.0
WINDOW = 128
PEER_TOPK = 16


def _params(*sem):
    return pltpu.CompilerParams(dimension_semantics=sem, vmem_limit_bytes=VMEM_LIMIT_BYTES)


def _dot(a, b):
    return jnp.dot(a, b, preferred_element_type=F32)


def _dot_nt(a, b):
    return lax.dot_general(a, b, (((1,), (1,)), ((), ())), preferred_element_type=F32)


def _softplus(z):
    return jnp.maximum(z, 0.0) + jnp.log(1.0 + jnp.exp(-jnp.abs(z)))


def _gelu(x):
    return 0.5 * x * (1.0 + lax.erf(x * (2.0 ** -0.5)))


def _layer_norm(z, g, b):
    mu = jnp.mean(z, axis=-1, keepdims=True)
    zc = z - mu
    var = jnp.mean(zc * zc, axis=-1, keepdims=True)
    return zc * lax.rsqrt(var + LN_EPS) * g + b


def _rms_norm(z, g):
    return z * lax.rsqrt(jnp.mean(z * z, axis=-1, keepdims=True) + RMS_EPS) * g


def _col_tile(n, cap=1024):
    best = None
    for t in range(LANES, min(n, cap) + 1, LANES):
        if n % t == 0:
            best = t
    return best if best is not None else n


def _matmul_kernel(x_ref, w_ref, o_ref):
    o_ref[...] = _dot(x_ref[...].astype(BF16), w_ref[...]).astype(o_ref.dtype)


def _matmul(x, w, *, tm, out_dtype=F32, name):
    m, k = x.shape
    n = w.shape[1]
    tn = _col_tile(n)
    return pl.pallas_call(
        _matmul_kernel,
        grid=(m // tm, n // tn),
        in_specs=[pl.BlockSpec((tm, k), lambda i, j: (i, 0)),
                  pl.BlockSpec((k, tn), lambda i, j: (0, j))],
        out_specs=pl.BlockSpec((tm, tn), lambda i, j: (i, j)),
        out_shape=jax.ShapeDtypeStruct((m, n), out_dtype),
        compiler_params=_params("parallel", "arbitrary"),
        name=name,
    )(x, w)


def _matmul_ln_kernel(a_ref, w_ref, res_ref, g_ref, b_ref, o_ref, *, alpha):
    y = _dot(a_ref[...].astype(BF16), w_ref[...])
    o_ref[...] = _layer_norm(alpha * res_ref[...] + y, g_ref[...], b_ref[...])


def _matmul_ln(a, w, res, g, b, *, alpha, tm, name):
    m, k = a.shape
    n = w.shape[1]
    return pl.pallas_call(
        functools.partial(_matmul_ln_kernel, alpha=alpha),
        grid=(m // tm,),
        in_specs=[pl.BlockSpec((tm, k), lambda i: (i, 0)),
                  pl.BlockSpec((k, n), lambda i: (0, 0)),
                  pl.BlockSpec((tm, n), lambda i: (i, 0)),
                  pl.BlockSpec((1, n), lambda i: (0, 0)),
                  pl.BlockSpec((1, n), lambda i: (0, 0))],
        out_specs=pl.BlockSpec((tm, n), lambda i: (i, 0)),
        out_shape=jax.ShapeDtypeStruct((m, n), F32),
        compiler_params=_params("parallel"),
        name=name,
    )(a, w, res, g.reshape(1, n), b.reshape(1, n))


def _head_matmul_kernel(x_ref, w_ref, o_ref):
    o_ref[...] = _dot(x_ref[...].astype(BF16), w_ref[...]).astype(o_ref.dtype)


def _head_matmul(x, w, *, out_dtype, name):
    h, k, n = w.shape
    m = x.shape[0]
    return pl.pallas_call(
        _head_matmul_kernel,
        grid=(h,),
        in_specs=[pl.BlockSpec((m, k), lambda i: (0, i)),
                  pl.BlockSpec((None, k, n), lambda i: (i, 0, 0))],
        out_specs=pl.BlockSpec((m, n), lambda i: (0, i)),
        out_shape=jax.ShapeDtypeStruct((m, h * n), out_dtype),
        compiler_params=_params("parallel"),
        name=name,
    )(x, w)


def _tri_ge(n):
    ri = lax.broadcasted_iota(jnp.int32, (n, n), 0)
    ci = lax.broadcasted_iota(jnp.int32, (n, n), 1)
    return jnp.where(ri >= ci, 1.0, 0.0).astype(BF16)


def _sb_blocks(q, k_cat, v_cat, u, c, acc, scale, mask, order):
    rows = q.shape[0]
    tk = u.shape[0]
    nblk = k_cat.shape[0] // tk
    blk = lambda x, n: x[:, n * tk:(n + 1) * tk]
    z = _dot_nt(q, k_cat) * scale
    lf = -_softplus(z)
    if mask is not None:
        lf = jnp.where(mask, lf, 0.0)
    hi = lf.astype(BF16)
    lo = (lf - hi.astype(F32)).astype(BF16)
    fold = lambda x: jnp.concatenate([blk(x, n) for n in range(nblk)], axis=0)
    cum = _dot(fold(hi), u) + _dot(fold(lo), u)
    offset = [None] * nblk
    for n in order:
        offset[n] = c
        c = c + jnp.sum(blk(lf, n), axis=-1, keepdims=True)
    a = jnp.concatenate([jnp.exp(blk(z, n) + cum[n * rows:(n + 1) * rows] + offset[n]) for n in range(nblk)],
                        axis=1)
    if mask is not None:
        a = jnp.where(mask, a, 0.0)
    acc = acc + _dot(a.astype(BF16), v_cat)
    return c, acc


def _sb_prompt_kernel(q_ref, k_ref, v_ref, o_ref, c_scr, acc_scr, *, tq, group, d, scale):
    qi = pl.program_id(2)
    rows = group * tq
    q = jnp.concatenate([q_ref[:, r * d:(r + 1) * d] for r in range(group)], axis=0).astype(BF16)
    u = _tri_ge(tq)

    def run(kb_first, nblk, mask):
        c = c_scr[...]
        acc = acc_scr[...]
        for n in range(nblk):
            start = pl.multiple_of((kb_first - n) * tq, tq)
            k = k_ref[pl.ds(start, tq), :].astype(BF16)
            v = v_ref[pl.ds(start, tq), :].astype(BF16)
            c, acc = _sb_blocks(q, k, v, u, c, acc, scale, mask, order=range(1))
        c_scr[...] = c
        acc_scr[...] = acc

    c_scr[...] = jnp.zeros_like(c_scr)
    acc_scr[...] = jnp.zeros_like(acc_scr)
    t_loc = lax.rem(lax.broadcasted_iota(jnp.int32, (rows, tq), 0), tq)
    s_loc = lax.broadcasted_iota(jnp.int32, (rows, tq), 1)
    run(qi, 1, s_loc < t_loc)

    one = qi & 1
    two = qi & 2

    @pl.when(one == 1)
    def _():
        run(qi - 1, 1, None)

    @pl.when(two == 2)
    def _():
        run(qi - 1 - one, 2, None)

    def body(i, carry):
        run(qi - 1 - one - two - 4 * i, 4, None)
        return carry

    lax.fori_loop(0, qi >> 2, body, 0)
    acc = acc_scr[...]
    for r in range(group):
        o_ref[:, r * d:(r + 1) * d] = acc[r * tq:(r + 1) * tq].astype(o_ref.dtype)


def _sb_prompt_attn(qkv, *, batch, seq, heads, kv_heads, d, tq=128):
    group = heads // kv_heads
    nq = seq // tq
    kcol = heads
    vcol = heads + kv_heads
    return pl.pallas_call(
        functools.partial(_sb_prompt_kernel, tq=tq, group=group, d=d, scale=d ** -0.5),
        grid=(batch, kv_heads, nq),
        in_specs=[pl.BlockSpec((tq, group * d), lambda b, g, i: (b * nq + i, g)),
                  pl.BlockSpec((seq, d), lambda b, g, i: (b, kcol + g)),
                  pl.BlockSpec((seq, d), lambda b, g, i: (b, vcol + g))],
        out_specs=pl.BlockSpec((tq, group * d), lambda b, g, i: (b * nq + i, g)),
        out_shape=jax.ShapeDtypeStruct((batch * seq, heads * d), BF16),
        scratch_shapes=[pltpu.VMEM((group * tq, 1), F32), pltpu.VMEM((group * tq, d), F32)],
        compiler_params=_params("parallel", "parallel", "arbitrary"),
        name="sb_prompt_attn",
    )(qkv, qkv, qkv)


def _block_diag_queries(q, group, d, width):
    heads = q.shape[0]
    qt = jnp.concatenate([q] * (width // d), axis=1)
    rg = lax.broadcasted_iota(jnp.int32, (heads, width), 0) // group
    lg = lax.broadcasted_iota(jnp.int32, (heads, width), 1) // d
    return jnp.where(rg == lg, qt, 0.0)


def _select_own_block(acc, group, d):
    heads, width = acc.shape
    rg = lax.broadcasted_iota(jnp.int32, (heads, d), 0) // group
    out = jnp.zeros((heads, d), F32)
    for g in range(width // d):
        out = out + jnp.where(rg == g, acc[:, g * d:(g + 1) * d], 0.0)
    return out


def _sb_sample_kernel(pt_ref, q_ref, kn_ref, vn_ref, *rest, pp, group, d, scale, n_past):
    k_refs = rest[:pp]
    v_refs = rest[pp:2 * pp]
    o_ref, c_scr, acc_scr = rest[2 * pp:]
    s = pl.program_id(1)
    heads = q_ref.shape[0]
    kv = heads // group
    page = k_refs[0].shape[0] // kv
    width = kv * d
    qbd = _block_diag_queries(q_ref[...], group, d, width)
    qbd_bf = qbd.astype(BF16)
    u = _tri_ge(page)

    def heads_to_lanes(ref):
        return jnp.concatenate([ref[pl.ds(g, page, stride=kv), :] for g in range(kv)], axis=1).astype(BF16)

    @pl.when(s == 0)
    def _():
        z = jnp.sum(qbd * kn_ref[...], axis=-1, keepdims=True) * scale
        q_pos = n_past + lax.broadcasted_iota(jnp.int32, (heads, 1), 1)
        k_pos = n_past + lax.broadcasted_iota(jnp.int32, (heads, 1), 1)
        vis = k_pos < q_pos
        lf = jnp.where(vis, -_softplus(z), 0.0)
        a = jnp.where(vis, jnp.exp(z + lf), 0.0)
        c_scr[...] = lf
        acc_scr[...] = a * jnp.broadcast_to(vn_ref[...], (heads, width))

    k_cat = jnp.concatenate([heads_to_lanes(k_refs[i]) for i in range(pp)], axis=0)
    v_cat = jnp.concatenate([heads_to_lanes(v_refs[i]) for i in range(pp)], axis=0)
    c, acc = _sb_blocks(qbd_bf, k_cat, v_cat, u, c_scr[...], acc_scr[...], scale, None, order=range(pp))
    c_scr[...] = c
    acc_scr[...] = acc

    @pl.when(s == pl.num_programs(1) - 1)
    def _():
        o_ref[...] = _select_own_block(acc, group, d).astype(o_ref.dtype)


def _sb_sample_attn(q, k_new, v_new, cache_k, cache_v, layer, page_table, *, group, d, page, pp):
    bsz, heads, _ = q.shape
    width = (heads // group) * d
    page_rows = cache_k.shape[2]
    n_pages = page_table.shape[1]
    steps = n_pages // pp

    def page_spec(i):
        return pl.BlockSpec((None, None, page_rows, d),
                            lambda b, s, pt: (layer, pt[b, n_pages - 1 - (s * pp + i)], 0, 0))

    grid_spec = pltpu.PrefetchScalarGridSpec(
        num_scalar_prefetch=1,
        grid=(bsz, steps),
        in_specs=[pl.BlockSpec((None, heads, d), lambda b, s, pt: (b, 0, 0)),
                  pl.BlockSpec((None, 1, width), lambda b, s, pt: (b, 0, 0)),
                  pl.BlockSpec((None, 1, width), lambda b, s, pt: (b, 0, 0))]
                 + [page_spec(i) for i in range(pp)] + [page_spec(i) for i in range(pp)],
        out_specs=pl.BlockSpec((None, heads, d), lambda b, s, pt: (b, 0, 0)),
        scratch_shapes=[pltpu.VMEM((heads, 1), F32), pltpu.VMEM((heads, width), F32)],
    )
    return pl.pallas_call(
        functools.partial(_sb_sample_kernel, pp=pp, group=group, d=d, scale=d ** -0.5,
                          n_past=n_pages * page),
        grid_spec=grid_spec,
        out_shape=jax.ShapeDtypeStruct((bsz, heads, d), BF16),
        compiler_params=_params("parallel", "arbitrary"),
        name="sb_sample_attn",
    )(page_table, q, k_new, v_new, *([cache_k] * pp), *([cache_v] * pp))


def _win_softmax(s, sink):
    m = jnp.maximum(jnp.max(s, axis=-1, keepdims=True), sink)
    p = jnp.exp(s - m)
    denom = jnp.sum(p, axis=-1, keepdims=True) + jnp.exp(sink - m)
    return p / denom


def _win_prompt_kernel(sink_ref, q_ref, kp_ref, kc_ref, vp_ref, vc_ref, o_ref, *, kv_heads, group, d, scale):
    nb = pl.program_id(1)
    tq = q_ref.shape[0]
    t_loc = lax.broadcasted_iota(jnp.int32, (tq, 2 * tq), 0)
    j_loc = lax.broadcasted_iota(jnp.int32, (tq, 2 * tq), 1)
    dist = t_loc + tq - j_loc
    k_pos = (nb - 1) * tq + j_loc
    valid = (dist >= 0) & (dist <= WINDOW) & (k_pos >= 0)
    dist_f = dist.astype(F32)
    for g in range(kv_heads):
        kb = jnp.concatenate([kp_ref[:, g * d:(g + 1) * d], kc_ref[:, g * d:(g + 1) * d]], axis=0).astype(BF16)
        vb = jnp.concatenate([vp_ref[:, g * d:(g + 1) * d], vc_ref[:, g * d:(g + 1) * d]], axis=0).astype(BF16)
        outs = []
        for r in range(group):
            h = g * group + r
            slope = 2.0 ** (-8.0 * (h + 1) / (kv_heads * group))
            qh = q_ref[:, h * d:(h + 1) * d].astype(BF16)
            s = _dot_nt(qh, kb) * scale
            s = jnp.where(valid, s - slope * dist_f, NEG_INF)
            p = _win_softmax(s, sink_ref[h])
            outs.append(_dot(p.astype(BF16), vb))
        for r in range(0, group, 2):
            h = g * group + r
            o_ref[:, h * d:(h + 2) * d] = jnp.concatenate(outs[r:r + 2], axis=-1).astype(o_ref.dtype)


def _win_prompt_attn(qkv, sinks, *, batch, seq, heads, kv_heads, d):
    tq = WINDOW
    nb = seq // tq
    group = heads // kv_heads
    qw = heads * d
    kvw = kv_heads * d
    kcol = qw // kvw
    vcol = kcol + 1
    grid_spec = pltpu.PrefetchScalarGridSpec(
        num_scalar_prefetch=0,
        grid=(batch, nb),
        in_specs=[pl.BlockSpec(memory_space=pltpu.SMEM),
                  pl.BlockSpec((tq, qw), lambda b, i: (b * nb + i, 0)),
                  pl.BlockSpec((tq, kvw), lambda b, i: (b * nb + jnp.maximum(i - 1, 0), kcol)),
                  pl.BlockSpec((tq, kvw), lambda b, i: (b * nb + i, kcol)),
                  pl.BlockSpec((tq, kvw), lambda b, i: (b * nb + jnp.maximum(i - 1, 0), vcol)),
                  pl.BlockSpec((tq, kvw), lambda b, i: (b * nb + i, vcol))],
        out_specs=pl.BlockSpec((tq, qw), lambda b, i: (b * nb + i, 0)),
    )
    return pl.pallas_call(
        functools.partial(_win_prompt_kernel, kv_heads=kv_heads, group=group, d=d, scale=d ** -0.5),
        grid_spec=grid_spec,
        out_shape=jax.ShapeDtypeStruct((batch * seq, qw), BF16),
        compiler_params=_params("parallel", "arbitrary"),
        name="win_prompt_attn",
    )(sinks, qkv, qkv, qkv, qkv, qkv)


def _win_sample_kernel(sink_ref, q_ref, kn_ref, vn_ref, kb_ref, vb_ref, o_ref, ko_ref, vo_ref,
                       *, group, d, scale, past_len):
    heads = q_ref.shape[0]
    n_buf, width = kb_ref.shape
    qbd = _block_diag_queries(q_ref[...], group, d, width)
    kbuf = kb_ref[...]
    vbuf = vb_ref[...]
    kn = kn_ref[...]
    vn = vn_ref[...]
    slope = jnp.exp2(-8.0 * (lax.broadcasted_iota(jnp.int32, (heads, 1), 0) + 1).astype(F32) / heads)
    q_pos = past_len
    j = lax.broadcasted_iota(jnp.int32, (heads, n_buf), 1)
    k_pos = past_len - n_buf + j
    dist = q_pos - k_pos
    valid = (dist >= 0) & (dist <= WINDOW) & (k_pos >= 0)
    s_buf = _dot_nt(qbd.astype(BF16), kbuf.astype(BF16)) * scale
    s_buf = jnp.where(valid, s_buf - slope * dist.astype(F32), NEG_INF)
    s_new = jnp.sum(qbd * kn, axis=-1, keepdims=True) * scale
    sink = sink_ref[...]
    m = jnp.maximum(jnp.maximum(jnp.max(s_buf, axis=-1, keepdims=True), s_new), sink)
    p_buf = jnp.exp(s_buf - m)
    p_new = jnp.exp(s_new - m)
    denom = jnp.sum(p_buf, axis=-1, keepdims=True) + p_new + jnp.exp(sink - m)
    acc = _dot((p_buf / denom).astype(BF16), vbuf.astype(BF16)) + (p_new / denom) * vn
    o_ref[...] = _select_own_block(acc, group, d).astype(o_ref.dtype)
    last = lax.broadcasted_iota(jnp.int32, (n_buf, width), 0) == n_buf - 1
    ko_ref[...] = jnp.where(last, kn, pltpu.roll(kbuf, n_buf - 1, 0))
    vo_ref[...] = jnp.where(last, vn, pltpu.roll(vbuf, n_buf - 1, 0))


def _win_sample_attn(q, k_new, v_new, buf_k, buf_v, sinks, *, group, d, past_len):
    bsz, heads, _ = q.shape
    _, n_buf, width = buf_k.shape
    row = lambda b: (b, 0, 0)
    return pl.pallas_call(
        functools.partial(_win_sample_kernel, group=group, d=d, scale=d ** -0.5, past_len=past_len),
        grid=(bsz,),
        in_specs=[pl.BlockSpec((heads, 1), lambda b: (0, 0)),
                  pl.BlockSpec((None, heads, d), row),
                  pl.BlockSpec((None, 1, width), row),
                  pl.BlockSpec((None, 1, width), row),
                  pl.BlockSpec((None, n_buf, width), row),
                  pl.BlockSpec((None, n_buf, width), row)],
        out_specs=[pl.BlockSpec((None, heads, d), row),
                   pl.BlockSpec((None, n_buf, width), row),
                   pl.BlockSpec((None, n_buf, width), row)],
        out_shape=[jax.ShapeDtypeStruct((bsz, heads, d), BF16),
                   jax.ShapeDtypeStruct((bsz, n_buf, width), F32),
                   jax.ShapeDtypeStruct((bsz, n_buf, width), F32)],
        compiler_params=_params("parallel"),
        name="win_sample_attn",
    )(sinks.reshape(heads, 1), q, k_new, v_new, buf_k, buf_v)


def _rope_tables(pos, rope_dim):
    inv_freq = ROPE_THETA ** (-jnp.arange(0, rope_dim, 2, dtype=F32) / rope_dim)
    ang = pos.astype(F32)[:, None] * inv_freq[None, :]
    cos, sin = jnp.cos(ang), jnp.sin(ang)
    reps = LANES // rope_dim
    return (jnp.concatenate([cos, cos] * reps, axis=-1),
            jnp.concatenate([-sin, sin] * reps, axis=-1))


def _rope_lanes(x, cos_t, sin_t, half):
    lane = lax.broadcasted_iota(jnp.int32, x.shape, 1)
    partner = jnp.where(lax.rem(lane, 2 * half) < half,
                        pltpu.roll(x, LANES - half, 1), pltpu.roll(x, half, 1))
    return x * cos_t + partner * sin_t


def _mla_down_kernel(x_ref, w_ref, qn_ref, kvn_ref, cos_ref, sin_ref, cq_ref, ckv_ref, kr_ref, kr2_ref,
                     *, q_lora, kv_lora, rope_dim):
    acc = _dot(x_ref[...].astype(BF16), w_ref[...])
    cq_ref[...] = _rms_norm(acc[:, :q_lora], qn_ref[...]).astype(cq_ref.dtype)
    ckv_ref[...] = _rms_norm(acc[:, q_lora:q_lora + kv_lora], kvn_ref[...])
    kr = acc[:, q_lora + kv_lora:]
    kr = _rope_lanes(kr, cos_ref[...], sin_ref[...], rope_dim // 2)
    lane = lax.broadcasted_iota(jnp.int32, kr.shape, 1)
    kr = jnp.where(lane < rope_dim, kr, 0.0)
    kr_ref[...] = kr[:, :rope_dim]
    kr2_ref[...] = (kr + pltpu.roll(kr, rope_dim, 1)).astype(kr2_ref.dtype)


def _mla_down(x, w_pad, q_norm, kv_norm, cos_t, sin_t, *, q_lora, kv_lora, rope_dim, tm, name):
    m, k = x.shape
    n = w_pad.shape[1]
    rowb = lambda w: pl.BlockSpec((tm, w), lambda i: (i, 0))
    full = lambda r, c: pl.BlockSpec((r, c), lambda i: (0, 0))
    return pl.pallas_call(
        functools.partial(_mla_down_kernel, q_lora=q_lora, kv_lora=kv_lora, rope_dim=rope_dim),
        grid=(m // tm,),
        in_specs=[rowb(k), full(k, n), full(1, q_lora), full(1, kv_lora), rowb(LANES), rowb(LANES)],
        out_specs=[rowb(q_lora), rowb(kv_lora), rowb(rope_dim), rowb(LANES)],
        out_shape=[jax.ShapeDtypeStruct((m, q_lora), BF16),
                   jax.ShapeDtypeStruct((m, kv_lora), F32),
                   jax.ShapeDtypeStruct((m, rope_dim), F32),
                   jax.ShapeDtypeStruct((m, LANES), BF16)],
        compiler_params=_params("parallel"),
        name=name,
    )(x, w_pad, q_norm.reshape(1, -1), kv_norm.reshape(1, -1), cos_t, sin_t)


def _mla_uq_kernel(cq_ref, w_ref, cos_ref, sin_ref, qn_ref, qr_ref, *, nope_w, rope_dim):
    acc = _dot(cq_ref[...], w_ref[...])
    qn_ref[...] = acc[:, :nope_w].astype(qn_ref.dtype)
    cos_t = cos_ref[...]
    sin_t = sin_ref[...]
    for c in range(qr_ref.shape[1] // LANES):
        x = acc[:, nope_w + c * LANES:nope_w + (c + 1) * LANES]
        qr_ref[:, c * LANES:(c + 1) * LANES] = _rope_lanes(x, cos_t, sin_t, rope_dim // 2).astype(qr_ref.dtype)


def _mla_uq(cq, w_perm, cos_t, sin_t, *, nope_w, rope_w, rope_dim, tm, name):
    m, k = cq.shape
    n = w_perm.shape[1]
    rowb = lambda w: pl.BlockSpec((tm, w), lambda i: (i, 0))
    return pl.pallas_call(
        functools.partial(_mla_uq_kernel, nope_w=nope_w, rope_dim=rope_dim),
        grid=(m // tm,),
        in_specs=[rowb(k), pl.BlockSpec((k, n), lambda i: (0, 0)), rowb(LANES), rowb(LANES)],
        out_specs=[rowb(nope_w), rowb(rope_w)],
        out_shape=[jax.ShapeDtypeStruct((m, nope_w), BF16), jax.ShapeDtypeStruct((m, rope_w), BF16)],
        compiler_params=_params("parallel"),
        name=name,
    )(cq, w_perm, cos_t, sin_t)


def _mla_prompt_kernel(qn_ref, qr_ref, kn_ref, kr_ref, v_ref, o_ref, *, tq, d, rope_dim, scale):
    qi = pl.program_id(2)
    lane = lax.broadcasted_iota(jnp.int32, (tq, LANES), 1)
    t_loc = lax.broadcasted_iota(jnp.int32, (tq, tq), 0)
    s_loc = lax.broadcasted_iota(jnp.int32, (tq, tq), 1)
    causal = s_loc <= t_loc
    qr_all = qr_ref[...]
    n_h = LANES // rope_dim
    qns = [qn_ref[:, hh * d:(hh + 1) * d] for hh in range(n_h)]
    qrs = [jnp.where((lane >= hh * rope_dim) & (lane < (hh + 1) * rope_dim), qr_all, 0.0).astype(BF16)
           for hh in range(n_h)]

    def block(kb, state, mask):
        start = pl.multiple_of(kb * tq, tq)
        kr = kr_ref[pl.ds(start, tq), :]
        out = []
        for hh in range(n_h):
            m, l, acc = state[hh]
            kn = kn_ref[pl.ds(start, tq), hh * d:(hh + 1) * d]
            v = v_ref[pl.ds(start, tq), hh * d:(hh + 1) * d]
            s = (_dot_nt(qns[hh], kn) + _dot_nt(qrs[hh], kr)) * scale
            if mask is not None:
                s = jnp.where(mask, s, NEG_INF)
            m_new = jnp.maximum(m, jnp.max(s, axis=-1, keepdims=True))
            alpha = jnp.exp(m - m_new)
            p = jnp.exp(s - m_new)
            l = alpha * l + jnp.sum(p, axis=-1, keepdims=True)
            acc = alpha * acc + _dot(p.astype(BF16), v)
            out.append((m_new, l, acc))
        return tuple(out)

    init = tuple((jnp.full((tq, 1), NEG_INF, F32), jnp.zeros((tq, 1), F32), jnp.zeros((tq, d), F32))
                 for _ in range(n_h))
    state = block(qi, init, causal)
    state = lax.fori_loop(0, qi, lambda i, st: block(i, st, None), state)
    for hh in range(n_h):
        _, l, acc = state[hh]
        o_ref[:, hh * d:(hh + 1) * d] = (acc / l).astype(o_ref.dtype)


def _mla_prompt_attn(q_nope, q_rope, kv, kr2, *, batch, seq, heads, d, rope_dim, tq=256):
    nq = seq // tq
    hp = LANES // rope_dim
    return pl.pallas_call(
        functools.partial(_mla_prompt_kernel, tq=tq, d=d, rope_dim=rope_dim, scale=(d + rope_dim) ** -0.5),
        grid=(batch, heads // hp, nq),
        in_specs=[pl.BlockSpec((tq, hp * d), lambda b, h, i: (b * nq + i, h)),
                  pl.BlockSpec((tq, LANES), lambda b, h, i: (b * nq + i, h)),
                  pl.BlockSpec((seq, hp * d), lambda b, h, i: (b, h)),
                  pl.BlockSpec((seq, LANES), lambda b, h, i: (b, 0)),
                  pl.BlockSpec((seq, hp * d), lambda b, h, i: (b, heads // hp + h))],
        out_specs=pl.BlockSpec((tq, hp * d), lambda b, h, i: (b * nq + i, h)),
        out_shape=jax.ShapeDtypeStruct((batch * seq, heads * d), BF16),
        compiler_params=_params("parallel", "parallel", "arbitrary"),
        name="mla_prompt_attn",
    )(q_nope, q_rope, kv, kr2, kv)


def _mla_sample_kernel(pt_ref, ql_ref, qr_ref, cn_ref, rn_ref, *rest, pp, scale, n_past):
    c_refs = rest[:pp]
    r_refs = rest[pp:2 * pp]
    o_ref, m_scr, l_scr, acc_scr = rest[2 * pp:]
    s_id = pl.program_id(1)
    heads, width = ql_ref.shape
    ql = ql_ref[...]
    qr = qr_ref[...]

    @pl.when(s_id == 0)
    def _():
        cn = cn_ref[...]
        s = (jnp.sum(ql * cn, axis=-1, keepdims=True) + jnp.sum(qr * rn_ref[...], axis=-1, keepdims=True)) * scale
        q_pos = n_past + lax.broadcasted_iota(jnp.int32, (heads, 1), 1)
        k_pos = n_past + lax.broadcasted_iota(jnp.int32, (heads, 1), 1)
        vis = k_pos <= q_pos
        m = jnp.where(vis, s, NEG_INF)
        p = jnp.where(vis, jnp.exp(s - m), 0.0)
        m_scr[...] = m
        l_scr[...] = p
        acc_scr[...] = p * jnp.broadcast_to(cn, (heads, width))

    ql_bf = ql.astype(BF16)
    qr_bf = qr.astype(BF16)
    m = m_scr[...]
    l = l_scr[...]
    acc = acc_scr[...]
    c_cat = jnp.concatenate([c_refs[i][...].astype(BF16) for i in range(pp)], axis=0)
    rt_cat = jnp.concatenate([r_refs[i][...].astype(BF16) for i in range(pp)], axis=1)
    s = (_dot_nt(ql_bf, c_cat) + _dot(qr_bf, rt_cat)) * scale
    m_new = jnp.maximum(m, jnp.max(s, axis=-1, keepdims=True))
    alpha = jnp.exp(m - m_new)
    p = jnp.exp(s - m_new)
    l = alpha * l + jnp.sum(p, axis=-1, keepdims=True)
    acc = alpha * acc + _dot(p.astype(BF16), c_cat)
    m_scr[...] = m_new
    l_scr[...] = l
    acc_scr[...] = acc

    @pl.when(s_id == pl.num_programs(1) - 1)
    def _():
        o_ref[...] = (acc / l).astype(o_ref.dtype)


def _mla_sample_attn(q_lat, q_rope, c_new, r_new, cache_c, cache_rt, layer, page_table, *, scale, pp):
    bsz, heads, width = q_lat.shape
    rope_dim = q_rope.shape[2]
    page = cache_c.shape[2]
    n_pages = page_table.shape[1]
    steps = n_pages // pp

    def page_spec(rows, cols, i):
        return pl.BlockSpec((None, None, rows, cols), lambda b, s, pt: (layer, pt[b, s * pp + i], 0, 0))

    row = lambda b, s, pt: (b, 0, 0)
    grid_spec = pltpu.PrefetchScalarGridSpec(
        num_scalar_prefetch=1,
        grid=(bsz, steps),
        in_specs=[pl.BlockSpec((None, heads, width), row),
                  pl.BlockSpec((None, heads, rope_dim), row),
                  pl.BlockSpec((None, 1, width), row),
                  pl.BlockSpec((None, 1, rope_dim), row)]
                 + [page_spec(page, width, i) for i in range(pp)]
                 + [page_spec(rope_dim, page, i) for i in range(pp)],
        out_specs=pl.BlockSpec((None, heads, width), row),
        scratch_shapes=[pltpu.VMEM((heads, 1), F32), pltpu.VMEM((heads, 1), F32),
                        pltpu.VMEM((heads, width), F32)],
    )
    return pl.pallas_call(
        functools.partial(_mla_sample_kernel, pp=pp, scale=scale, n_past=n_pages * page),
        grid_spec=grid_spec,
        out_shape=jax.ShapeDtypeStruct((bsz, heads, width), BF16),
        compiler_params=_params("parallel", "arbitrary"),
        name="mla_sample_attn",
    )(page_table, q_lat, q_rope, c_new, r_new, *([cache_c] * pp), *([cache_rt] * pp))


def _top_values(x, k):
    tops = []
    for _ in range(k):
        m = jnp.max(x, axis=0, keepdims=True)
        tops.append(m)
        x = jnp.where(x == m, -jnp.inf, x)
    return jnp.concatenate(tops, axis=0)


def _peer_route_kernel(x_ref, wq_ref, sk_ref, th_ref, sb_ref, ea_ref, eb_ref, *, heads, half):
    q = _dot(x_ref[...].astype(BF16), wq_ref[...]).astype(BF16)
    for h in range(heads):
        qa = q[:, (2 * h) * half:(2 * h + 1) * half]
        qb = q[:, (2 * h + 1) * half:(2 * h + 2) * half]
        sa = _dot_nt(sk_ref[0], qa)
        sb = _dot_nt(sk_ref[1], qb)
        k1 = PEER_TOPK + 1
        ta = _top_values(sa, k1)
        tb = _top_values(sb, k1)
        cand = jnp.concatenate([ta[i:i + 1] + tb[:k1 // (i + 1)] for i in range(k1)], axis=0)
        tc = _top_values(cand, k1)
        tau = 0.5 * (tc[PEER_TOPK - 1:PEER_TOPK] + tc[PEER_TOPK:k1])
        best = ta[0:1] + tb[0:1]
        z = jnp.sum(jnp.where(cand > tau, jnp.exp(cand - best), 0.0), axis=0, keepdims=True)
        th_ref[h] = tau - sa
        sb_ref[h] = sb
        ea_ref[h] = jnp.exp(sa - ta[0:1]) / z
        eb_ref[h] = jnp.exp(sb - tb[0:1])


def _peer_route(x, w_q, subkeys, *, heads, tn, name):
    n, dm = x.shape
    _, keys, half = subkeys.shape
    big = pl.BlockSpec((heads, keys, tn), lambda i: (0, 0, i))
    big_shape = jax.ShapeDtypeStruct((heads, keys, n), F32)
    return pl.pallas_call(
        functools.partial(_peer_route_kernel, heads=heads, half=half),
        grid=(n // tn,),
        in_specs=[pl.BlockSpec((tn, dm), lambda i: (i, 0)),
                  pl.BlockSpec(w_q.shape, lambda i: (0, 0)),
                  pl.BlockSpec(subkeys.shape, lambda i: (0, 0, 0))],
        out_specs=[big, big, big, big],
        out_shape=[big_shape, big_shape, big_shape, big_shape],
        compiler_params=_params("parallel"),
        name=name,
    )(x, w_q, subkeys)


def _peer_dense_kernel(x_ref, th_ref, sb_ref, ea_ref, eb_ref, u_ref, vt_ref, g_ref, b_ref, o_ref,
                       xt_scr, yt_scr, pt_even, pt_odd, *, heads, keys, alpha):
    j = pl.program_id(1)
    tn = x_ref.shape[0]
    ce = u_ref.shape[0]

    @pl.when(j == 0)
    def _():
        xt_scr[...] = x_ref[...].T.astype(BF16)
        yt_scr[...] = jnp.zeros_like(yt_scr)
        pt_odd[...] = jnp.zeros_like(pt_odd)

    def step(pt_cur, pt_prev):
        ht = _dot(u_ref[...], xt_scr[...])
        yt_scr[...] += _dot(vt_ref[...], pt_prev[...])
        for ai in range(ce // keys):
            for tt in range(tn // LANES):
                tok = slice(tt * LANES, (tt + 1) * LANES)
                gate = jnp.zeros((keys, LANES), F32)
                for h in range(heads):
                    chosen = sb_ref[h, :, tok] > th_ref[h, ai:ai + 1, tok]
                    gate = gate + jnp.where(chosen, ea_ref[h, ai:ai + 1, tok] * eb_ref[h, :, tok], 0.0)
                hv = ht[ai * keys:(ai + 1) * keys, tok]
                pt_cur[ai * keys:(ai + 1) * keys, tok] = (gate * _gelu(hv)).astype(BF16)

    @pl.when((j & 1) == 0)
    def _():
        step(pt_even, pt_odd)

    @pl.when((j & 1) == 1)
    def _():
        step(pt_odd, pt_even)

    @pl.when(j == pl.num_programs(1) - 1)
    def _():
        o_ref[...] = _layer_norm(alpha * x_ref[...] + yt_scr[...].T, g_ref[...], b_ref[...])


def _peer_dense(x, route, u_bf, vt_bf, g, b, *, alpha, tn, ce, name):
    th, sb, ea, eb = route
    n, dm = x.shape
    heads, keys, _ = sb.shape
    n_tiles = u_bf.shape[0] // ce
    once = pl.Buffered(1)
    cur = lambda j: jnp.minimum(j, n_tiles - 1)
    prev = lambda j: jnp.maximum(j - 1, 0)
    by_b = pl.BlockSpec((heads, keys, tn), lambda i, j: (0, 0, i), pipeline_mode=once)
    by_a = pl.BlockSpec((heads, ce // keys, tn), lambda i, j: (0, cur(j), i))
    return pl.pallas_call(
        functools.partial(_peer_dense_kernel, heads=heads, keys=keys, alpha=alpha),
        grid=(n // tn, n_tiles + 1),
        in_specs=[pl.BlockSpec((tn, dm), lambda i, j: (i, 0), pipeline_mode=once),
                  by_a, by_b, by_a, by_b,
                  pl.BlockSpec((ce, dm), lambda i, j: (cur(j), 0)),
                  pl.BlockSpec((dm, ce), lambda i, j: (0, prev(j))),
                  pl.BlockSpec((1, dm), lambda i, j: (0, 0)),
                  pl.BlockSpec((1, dm), lambda i, j: (0, 0))],
        out_specs=pl.BlockSpec((tn, dm), lambda i, j: (i, 0)),
        out_shape=jax.ShapeDtypeStruct((n, dm), F32),
        scratch_shapes=[pltpu.VMEM((dm, tn), BF16), pltpu.VMEM((dm, tn), F32),
                        pltpu.VMEM((ce, tn), BF16), pltpu.VMEM((ce, tn), BF16)],
        compiler_params=_params("parallel", "arbitrary"),
        name=name,
    )(x, th, sb, ea, eb, u_bf, vt_bf, g.reshape(1, dm), b.reshape(1, dm))


def _peer_layer(x, w_q, subkeys, u_bf, vt_bf, g, b, *, alpha, heads, tn_route, tn_dense, ce, tag):
    route = _peer_route(x, w_q, subkeys, heads=heads, tn=tn_route, name="peer_route_" + tag)
    return _peer_dense(x, route, u_bf, vt_bf, g, b, alpha=alpha, tn=tn_dense, ce=ce, name="peer_dense_" + tag)


def kernel(x_prompt, x_sample, cache_sb_k, cache_sb_v, cache_win_k, cache_win_v, cache_mla_ckv, cache_mla_krope, page_table, sb_w_qkv, sb_w_o, win_w_qkv, win_w_o, win_sinks, mla_w_down, mla_q_norm, mla_kv_norm, mla_w_uq, mla_w_uk, mla_w_uv, mla_w_o, peer_w_q, peer_subkeys, peer_u, peer_v, ln_mix_g, ln_mix_b, ln_ffn_g, ln_ffn_b):
    batch, seq, dm = x_prompt.shape
    bsz, dec_seq, _ = x_sample.shape
    assert dec_seq == 1
    depth = peer_w_q.shape[0]
    alpha = (2.0 * depth) ** 0.25
    page = cache_sb_k.shape[2]
    n_pages = page_table.shape[1]
    past_len = n_pages * page

    sb_kv, sb_d = cache_sb_k.shape[3], cache_sb_k.shape[4]
    sb_heads = dm // sb_d
    win_kv, win_d = cache_win_k.shape[3], cache_win_k.shape[4]
    win_heads = win_sinks.shape[1]
    q_lora = mla_q_norm.shape[1]
    kv_lora = mla_kv_norm.shape[1]
    rope_dim = cache_mla_krope.shape[3]
    mla_heads, mla_nope = mla_w_uk.shape[2], mla_w_uk.shape[3]
    mla_v = mla_w_uv.shape[3]
    peer_heads = peer_w_q.shape[2] // (2 * peer_subkeys.shape[3])

    n_p = batch * seq
    tm_p = 512 if n_p % 512 == 0 else n_p
    tm_ln = 256 if n_p % 256 == 0 else n_p
    tn_route_p = 256 if n_p % 256 == 0 else n_p
    tn_dense_p = 512 if n_p % 512 == 0 else n_p

    xp = x_prompt.reshape(n_p, dm)
    xs = x_sample.reshape(bsz, dm)

    cache_sb_k2 = cache_sb_k.reshape(cache_sb_k.shape[:2] + (page * sb_kv, sb_d))
    cache_sb_v2 = cache_sb_v.reshape(cache_sb_v.shape[:2] + (page * sb_kv, sb_d))
    pages_per_step = 8 if n_pages % 8 == 0 else 1

    cos_p, sin_p = _rope_tables(jnp.arange(seq), rope_dim)
    cos_p = jnp.tile(cos_p, (batch, 1))
    sin_p = jnp.tile(sin_p, (batch, 1))
    cos_s, sin_s = _rope_tables(jnp.full((bsz,), past_len), rope_dim)

    outs = {k: [] for k in ("sb_kp", "sb_vp", "sb_ks", "sb_vs", "win_kp", "win_vp", "win_ks", "win_vs",
                            "mla_cp", "mla_rp", "mla_cs", "mla_rs")}

    for i in range(depth):
        li, kind = divmod(i, 3)
        g_mix, b_mix = ln_mix_g[i], ln_mix_b[i]
        if kind == 0:
            w_qkv = sb_w_qkv[li].astype(BF16)
            w_o = sb_w_o[li].astype(BF16)
            qw, kvw = sb_heads * sb_d, sb_kv * sb_d
            qkv_p = _matmul(xp, w_qkv, tm=tm_p, name="sb_qkv_p")
            qkv_s = _matmul(xs, w_qkv, tm=bsz, name="sb_qkv_s")
            o_p = _sb_prompt_attn(qkv_p, batch=batch, seq=seq, heads=sb_heads, kv_heads=sb_kv, d=sb_d)
            k_s = qkv_s[:, qw:qw + kvw]
            v_s = qkv_s[:, qw + kvw:]
            o_s = _sb_sample_attn(qkv_s[:, :qw].reshape(bsz, sb_heads, sb_d), k_s.reshape(bsz, 1, kvw),
                                  v_s.reshape(bsz, 1, kvw), cache_sb_k2, cache_sb_v2, li, page_table,
                                  group=sb_heads // sb_kv, d=sb_d, page=page, pp=pages_per_step)
            xp = _matmul_ln(o_p, w_o, xp, g_mix, b_mix, alpha=alpha, tm=tm_ln, name="sb_out_p")
            xs = _matmul_ln(o_s.reshape(bsz, qw), w_o, xs, g_mix, b_mix, alpha=alpha, tm=bsz, name="sb_out_s")
            outs["sb_kp"].append(qkv_p[:, qw:qw + kvw].reshape(batch, seq, sb_kv, sb_d))
            outs["sb_vp"].append(qkv_p[:, qw + kvw:].reshape(batch, seq, sb_kv, sb_d))
            outs["sb_ks"].append(k_s.reshape(bsz, 1, sb_kv, sb_d))
            outs["sb_vs"].append(v_s.reshape(bsz, 1, sb_kv, sb_d))
        elif kind == 1:
            w_qkv = win_w_qkv[li].astype(BF16)
            w_o = win_w_o[li].astype(BF16)
            qw, kvw = win_heads * win_d, win_kv * win_d
            qkv_p = _matmul(xp, w_qkv, tm=tm_p, name="win_qkv_p")
            qkv_s = _matmul(xs, w_qkv, tm=bsz, name="win_qkv_s")
            o_p = _win_prompt_attn(qkv_p, win_sinks[li], batch=batch, seq=seq, heads=win_heads,
                                   kv_heads=win_kv, d=win_d)
            n_buf = cache_win_k.shape[2]
            o_s, nk_s, nv_s = _win_sample_attn(
                qkv_s[:, :qw].reshape(bsz, win_heads, win_d), qkv_s[:, qw:qw + kvw].reshape(bsz, 1, kvw),
                qkv_s[:, qw + kvw:].reshape(bsz, 1, kvw), cache_win_k[li].reshape(bsz, n_buf, kvw),
                cache_win_v[li].reshape(bsz, n_buf, kvw), win_sinks[li],
                group=win_heads // win_kv, d=win_d, past_len=past_len)
            xp = _matmul_ln(o_p, w_o, xp, g_mix, b_mix, alpha=alpha, tm=tm_ln, name="win_out_p")
            xs = _matmul_ln(o_s.reshape(bsz, qw), w_o, xs, g_mix, b_mix, alpha=alpha, tm=bsz, name="win_out_s")
            keep = min(WINDOW, seq)
            kp = qkv_p[:, qw:qw + kvw].reshape(batch, seq, win_kv, win_d)
            vp = qkv_p[:, qw + kvw:].reshape(batch, seq, win_kv, win_d)
            outs["win_kp"].append(kp[:, seq - keep:])
            outs["win_vp"].append(vp[:, seq - keep:])
            outs["win_ks"].append(nk_s.reshape(bsz, n_buf, win_kv, win_d))
            outs["win_vs"].append(nv_s.reshape(bsz, n_buf, win_kv, win_d))
        else:
            down_w = q_lora + kv_lora + rope_dim
            w_down = jnp.pad(mla_w_down[li], ((0, 0), (0, q_lora + kv_lora + LANES - down_w))).astype(BF16)
            w_uq = mla_w_uq[li].reshape(q_lora, mla_heads, mla_nope + rope_dim)
            w_uq = jnp.concatenate([w_uq[:, :, :mla_nope].reshape(q_lora, -1),
                                    w_uq[:, :, mla_nope:].reshape(q_lora, -1)], axis=1).astype(BF16)
            nope_w, rope_w = mla_heads * mla_nope, mla_heads * rope_dim
            w_kv = jnp.concatenate([mla_w_uk[li].reshape(kv_lora, -1), mla_w_uv[li].reshape(kv_lora, -1)],
                                   axis=1).astype(BF16)
            w_uk_t = jnp.transpose(mla_w_uk[li], (1, 2, 0)).astype(BF16)
            w_uv_h = jnp.transpose(mla_w_uv[li], (1, 0, 2)).astype(BF16)
            w_o = mla_w_o[li].astype(BF16)
            mla_kw = dict(q_lora=q_lora, kv_lora=kv_lora, rope_dim=rope_dim)
            uq_kw = dict(nope_w=nope_w, rope_w=rope_w, rope_dim=rope_dim)
            scale = (mla_nope + rope_dim) ** -0.5

            cq_p, ckv_p, kr_p, kr2_p = _mla_down(xp, w_down, mla_q_norm[li], mla_kv_norm[li], cos_p, sin_p,
                                                 tm=tm_p, name="mla_down_p", **mla_kw)
            qn_p, qr_p = _mla_uq(cq_p, w_uq, cos_p, sin_p, tm=tm_p, name="mla_uq_p", **uq_kw)
            kv_p = _matmul(ckv_p, w_kv, tm=tm_p, out_dtype=BF16, name="mla_kv_p")
            o_p = _mla_prompt_attn(qn_p, qr_p, kv_p, kr2_p, batch=batch, seq=seq, heads=mla_heads,
                                   d=mla_nope, rope_dim=rope_dim, tq=256 if seq % 256 == 0 else seq)
            xp = _matmul_ln(o_p, w_o, xp, g_mix, b_mix, alpha=alpha, tm=tm_ln, name="mla_out_p")

            cq_s, ckv_s, kr_s, _ = _mla_down(xs, w_down, mla_q_norm[li], mla_kv_norm[li], cos_s, sin_s,
                                             tm=bsz, name="mla_down_s", **mla_kw)
            qn_s, qr_s = _mla_uq(cq_s, w_uq, cos_s, sin_s, tm=bsz, name="mla_uq_s", **uq_kw)
            q_lat = _head_matmul(qn_s, w_uk_t, out_dtype=F32, name="mla_qlat_s")
            o_lat = _mla_sample_attn(q_lat.reshape(bsz, mla_heads, kv_lora),
                                     qr_s.astype(F32).reshape(bsz, mla_heads, rope_dim),
                                     ckv_s.reshape(bsz, 1, kv_lora), kr_s.reshape(bsz, 1, rope_dim),
                                     cache_mla_ckv, jnp.swapaxes(cache_mla_krope, 2, 3), li, page_table,
                                     scale=scale, pp=pages_per_step)
            o_s = _head_matmul(o_lat.reshape(bsz, mla_heads * kv_lora), w_uv_h, out_dtype=BF16, name="mla_ov_s")
            xs = _matmul_ln(o_s, w_o, xs, g_mix, b_mix, alpha=alpha, tm=bsz, name="mla_out_s")
            outs["mla_cp"].append(ckv_p.reshape(batch, seq, kv_lora))
            outs["mla_rp"].append(kr_p.reshape(batch, seq, rope_dim))
            outs["mla_cs"].append(ckv_s.reshape(bsz, 1, kv_lora))
            outs["mla_rs"].append(kr_s.reshape(bsz, 1, rope_dim))

        w_q = peer_w_q[i].astype(BF16)
        subkeys = peer_subkeys[i].astype(BF16)
        u_bf = peer_u[i].astype(BF16)
        vt_bf = peer_v[i].astype(BF16).T
        peer_kw = dict(alpha=alpha, heads=peer_heads, ce=8 * peer_subkeys.shape[2])
        xp = _peer_layer(xp, w_q, subkeys, u_bf, vt_bf, ln_ffn_g[i], ln_ffn_b[i], tn_route=tn_route_p,
                         tn_dense=tn_dense_p, tag="p", **peer_kw)
        xs = _peer_layer(xs, w_q, subkeys, u_bf, vt_bf, ln_ffn_g[i], ln_ffn_b[i], tn_route=bsz,
                         tn_dense=bsz, tag="s", **peer_kw)

    stack = lambda k: jnp.stack(outs[k])
    return (xp.reshape(batch, seq, dm), xs.reshape(bsz, 1, dm),
            stack("sb_kp"), stack("sb_vp"), stack("sb_ks"), stack("sb_vs"),
            stack("win_kp"), stack("win_vp"), stack("win_ks"), stack("win_vs"),
            stack("mla_cp"), stack("mla_rp"), stack("mla_cs"), stack("mla_rs"))
```

```python
import functools
import math

import jax
import jax.numpy as jnp
from jax import lax
from jax.experimental import pallas as pl
from jax.experimental.pallas import tpu as pltpu

F32 = jnp.float32
BF16 = jnp.bfloat16

LANES = 128
VMEM_LIMIT_BYTES = 56 * 1024 * 1024

NEG_INF = -1e30
LN_EPS = 1e-5
RMS_EPS = 1e-6
ROPE_THETA = 10000.0
WINDOW = 128
PEER_TOPK = 16


def _params(*sem):
    return pltpu.CompilerParams(dimension_semantics=sem, vmem_limit_bytes=VMEM_LIMIT_BYTES)


def _dot(a, b):
    return jnp.dot(a, b, preferred_element_type=F32)


def _dot_nt(a, b):
    return lax.dot_general(a, b, (((1,), (1,)), ((), ())), preferred_element_type=F32)


def _softplus(z):
    return jnp.maximum(z, 0.0) + jnp.log(1.0 + jnp.exp(-jnp.abs(z)))


def _gelu(x):
    return 0.5 * x * (1.0 + lax.erf(x * (2.0 ** -0.5)))


def _layer_norm(z, g, b):
    mu = jnp.mean(z, axis=-1, keepdims=True)
    zc = z - mu
    var = jnp.mean(zc * zc, axis=-1, keepdims=True)
    return zc * lax.rsqrt(var + LN_EPS) * g + b


def _rms_norm(z, g):
    return z * lax.rsqrt(jnp.mean(z * z, axis=-1, keepdims=True) + RMS_EPS) * g


def _col_tile(n, cap=1024):
    best = None
    for t in range(LANES, min(n, cap) + 1, LANES):
        if n % t == 0:
            best = t
    return best if best is not None else n


def _matmul_kernel(x_ref, w_ref, o_ref):
    o_ref[...] = _dot(x_ref[...].astype(BF16), w_ref[...]).astype(o_ref.dtype)


def _matmul(x, w, *, tm, out_dtype=F32, name):
    m, k = x.shape
    n = w.shape[1]
    tn = _col_tile(n)
    return pl.pallas_call(
        _matmul_kernel,
        grid=(m // tm, n // tn),
        in_specs=[pl.BlockSpec((tm, k), lambda i, j: (i, 0)),
                  pl.BlockSpec((k, tn), lambda i, j: (0, j))],
        out_specs=pl.BlockSpec((tm, tn), lambda i, j: (i, j)),
        out_shape=jax.ShapeDtypeStruct((m, n), out_dtype),
        compiler_params=_params("parallel", "arbitrary"),
        name=name,
    )(x, w)


def _matmul_ln_kernel(a_ref, w_ref, res_ref, g_ref, b_ref, o_ref, *, alpha):
    y = _dot(a_ref[...].astype(BF16), w_ref[...])
    o_ref[...] = _layer_norm(alpha * res_ref[...] + y, g_ref[...], b_ref[...])


def _matmul_ln(a, w, res, g, b, *, alpha, tm, name):
    m, k = a.shape
    n = w.shape[1]
    return pl.pallas_call(
        functools.partial(_matmul_ln_kernel, alpha=alpha),
        grid=(m // tm,),
        in_specs=[pl.BlockSpec((tm, k), lambda i: (i, 0)),
                  pl.BlockSpec((k, n), lambda i: (0, 0)),
                  pl.BlockSpec((tm, n), lambda i: (i, 0)),
                  pl.BlockSpec((1, n), lambda i: (0, 0)),
                  pl.BlockSpec((1, n), lambda i: (0, 0))],
        out_specs=pl.BlockSpec((tm, n), lambda i: (i, 0)),
        out_shape=jax.ShapeDtypeStruct((m, n), F32),
        compiler_params=_params("parallel"),
        name=name,
    )(a, w, res, g.reshape(1, n), b.reshape(1, n))


def _head_matmul_kernel(x_ref, w_ref, o_ref):
    o_ref[...] = _dot(x_ref[...].astype(BF16), w_ref[...]).astype(o_ref.dtype)


def _head_matmul(x, w, *, out_dtype, name):
    h, k, n = w.shape
    m = x.shape[0]
    return pl.pallas_call(
        _head_matmul_kernel,
        grid=(h,),
        in_specs=[pl.BlockSpec((m, k), lambda i: (0, i)),
                  pl.BlockSpec((None, k, n), lambda i: (i, 0, 0))],
        out_specs=pl.BlockSpec((m, n), lambda i: (0, i)),
        out_shape=jax.ShapeDtypeStruct((m, h * n), out_dtype),
        compiler_params=_params("parallel"),
        name=name,
    )(x, w)


def _tri_ge(n):
    ri = lax.broadcasted_iota(jnp.int32, (n, n), 0)
    ci = lax.broadcasted_iota(jnp.int32, (n, n), 1)
    return jnp.where(ri >= ci, 1.0, 0.0).astype(BF16)


def _sb_blocks(q, k_cat, v_cat, u, c, acc, scale, mask, order):
    rows = q.shape[0]
    tk = u.shape[0]
    nblk = k_cat.shape[0] // tk
    blk = lambda x, n: x[:, n * tk:(n + 1) * tk]
    z = _dot_nt(q, k_cat) * scale
    lf = -_softplus(z)
    if mask is not None:
        lf = jnp.where(mask, lf, 0.0)
    hi = lf.astype(BF16)
    lo = (lf - hi.astype(F32)).astype(BF16)
    fold = lambda x: jnp.concatenate([blk(x, n) for n in range(nblk)], axis=0)
    cum = _dot(fold(hi), u) + _dot(fold(lo), u)
    offset = [None] * nblk
    for n in order:
        offset[n] = c
        c = c + jnp.sum(blk(lf, n), axis=-1, keepdims=True)
    a = jnp.concatenate([jnp.exp(blk(z, n) + cum[n * rows:(n + 1) * rows] + offset[n]) for n in range(nblk)],
                        axis=1)
    if mask is not None:
        a = jnp.where(mask, a, 0.0)
    acc = acc + _dot(a.astype(BF16), v_cat)
    return c, acc


def _sb_prompt_kernel(q_ref, k_ref, v_ref, o_ref, c_scr, acc_scr, *, tq, group, d, scale):
    qi = pl.program_id(2)
    rows = group * tq
    q = jnp.concatenate([q_ref[:, r * d:(r + 1) * d] for r in range(group)], axis=0).astype(BF16)
    u = _tri_ge(tq)

    def run(kb_first, nblk, mask):
        c = c_scr[...]
        acc = acc_scr[...]
        for n in range(nblk):
            start = pl.multiple_of((kb_first - n) * tq, tq)
            k = k_ref[pl.ds(start, tq), :].astype(BF16)
            v = v_ref[pl.ds(start, tq), :].astype(BF16)
            c, acc = _sb_blocks(q, k, v, u, c, acc, scale, mask, order=range(1))
        c_scr[...] = c
        acc_scr[...] = acc

    c_scr[...] = jnp.zeros_like(c_scr)
    acc_scr[...] = jnp.zeros_like(acc_scr)
    t_loc = lax.rem(lax.broadcasted_iota(jnp.int32, (rows, tq), 0), tq)
    s_loc = lax.broadcasted_iota(jnp.int32, (rows, tq), 1)
    run(qi, 1, s_loc < t_loc)

    one = qi & 1
    two = qi & 2
    four = qi & 4

    @pl.when(one == 1)
    def _():
        run(qi - 1, 1, None)

    @pl.when(two == 2)
    def _():
        run(qi - 1 - one, 2, None)

    @pl.when(four == 4)
    def _():
        run(qi - 1 - one - two, 4, None)

    def body(i, carry):
        run(qi - 1 - one - two - four - 8 * i, 8, None)
        return carry

    lax.fori_loop(0, qi >> 3, body, 0)
    acc = acc_scr[...]
    for r in range(group):
        o_ref[:, r * d:(r + 1) * d] = acc[r * tq:(r + 1) * tq].astype(o_ref.dtype)


def _sb_prompt_attn(qkv, *, batch, seq, heads, kv_heads, d, tq=128):
    group = heads // kv_heads
    nq = seq // tq
    kcol = heads
    vcol = heads + kv_heads
    return pl.pallas_call(
        functools.partial(_sb_prompt_kernel, tq=tq, group=group, d=d, scale=d ** -0.5),
        grid=(batch, kv_heads, nq),
        in_specs=[pl.BlockSpec((tq, group * d), lambda b, g, i: (b * nq + i, g)),
                  pl.BlockSpec((seq, d), lambda b, g, i: (b, kcol + g)),
                  pl.BlockSpec((seq, d), lambda b, g, i: (b, vcol + g))],
        out_specs=pl.BlockSpec((tq, group * d), lambda b, g, i: (b * nq + i, g)),
        out_shape=jax.ShapeDtypeStruct((batch * seq, heads * d), BF16),
        scratch_shapes=[pltpu.VMEM((group * tq, 1), F32), pltpu.VMEM((group * tq, d), F32)],
        compiler_params=_params("parallel", "parallel", "arbitrary"),
        name="sb_prompt_attn",
    )(qkv, qkv, qkv)


def _block_diag_queries(q, group, d, width):
    heads = q.shape[0]
    qt = jnp.concatenate([q] * (width // d), axis=1)
    rg = lax.broadcasted_iota(jnp.int32, (heads, width), 0) // group
    lg = lax.broadcasted_iota(jnp.int32, (heads, width), 1) // d
    return jnp.where(rg == lg, qt, 0.0)


def _select_own_block(acc, group, d):
    heads, width = acc.shape
    rg = lax.broadcasted_iota(jnp.int32, (heads, d), 0) // group
    out = jnp.zeros((heads, d), F32)
    for g in range(width // d):
        out = out + jnp.where(rg == g, acc[:, g * d:(g + 1) * d], 0.0)
    return out


def _sb_sample_kernel(pt_ref, q_ref, kn_ref, vn_ref, *rest, pp, group, d, scale, n_past):
    k_refs = rest[:pp]
    v_refs = rest[pp:2 * pp]
    o_ref, c_scr, acc_scr = rest[2 * pp:]
    s = pl.program_id(1)
    heads = q_ref.shape[0]
    kv = heads // group
    page = k_refs[0].shape[0] // kv
    width = kv * d
    qbd = _block_diag_queries(q_ref[...], group, d, width)
    qbd_bf = qbd.astype(BF16)
    u = _tri_ge(page)

    def heads_to_lanes(ref):
        return jnp.concatenate([ref[pl.ds(g, page, stride=kv), :] for g in range(kv)], axis=1).astype(BF16)

    @pl.when(s == 0)
    def _():
        z = jnp.sum(qbd * kn_ref[...], axis=-1, keepdims=True) * scale
        q_pos = n_past + lax.broadcasted_iota(jnp.int32, (heads, 1), 1)
        k_pos = n_past + lax.broadcasted_iota(jnp.int32, (heads, 1), 1)
        vis = k_pos < q_pos
        lf = jnp.where(vis, -_softplus(z), 0.0)
        a = jnp.where(vis, jnp.exp(z + lf), 0.0)
        c_scr[...] = lf
        acc_scr[...] = a * jnp.broadcast_to(vn_ref[...], (heads, width))

    k_cat = jnp.concatenate([heads_to_lanes(k_refs[i]) for i in range(pp)], axis=0)
    v_cat = jnp.concatenate([heads_to_lanes(v_refs[i]) for i in range(pp)], axis=0)
    c, acc = _sb_blocks(qbd_bf, k_cat, v_cat, u, c_scr[...], acc_scr[...], scale, None, order=range(pp))
    c_scr[...] = c
    acc_scr[...] = acc

    @pl.when(s == pl.num_programs(1) - 1)
    def _():
        o_ref[...] = _select_own_block(acc, group, d).astype(o_ref.dtype)


def _sb_sample_attn(q, k_new, v_new, cache_k, cache_v, layer, page_table, *, group, d, page, pp):
    bsz, heads, _ = q.shape
    width = (heads // group) * d
    page_rows = cache_k.shape[2]
    n_pages = page_table.shape[1]
    steps = n_pages // pp

    def page_spec(i):
        return pl.BlockSpec((None, None, page_rows, d),
                            lambda b, s, pt: (layer, pt[b, n_pages - 1 - (s * pp + i)], 0, 0))

    grid_spec = pltpu.PrefetchScalarGridSpec(
        num_scalar_prefetch=1,
        grid=(bsz, steps),
        in_specs=[pl.BlockSpec((None, heads, d), lambda b, s, pt: (b, 0, 0)),
                  pl.BlockSpec((None, 1, width), lambda b, s, pt: (b, 0, 0)),
                  pl.BlockSpec((None, 1, width), lambda b, s, pt: (b, 0, 0))]
                 + [page_spec(i) for i in range(pp)] + [page_spec(i) for i in range(pp)],
        out_specs=pl.BlockSpec((None, heads, d), lambda b, s, pt: (b, 0, 0)),
        scratch_shapes=[pltpu.VMEM((heads, 1), F32), pltpu.VMEM((heads, width), F32)],
    )
    return pl.pallas_call(
        functools.partial(_sb_sample_kernel, pp=pp, group=group, d=d, scale=d ** -0.5,
                          n_past=n_pages * page),
        grid_spec=grid_spec,
        out_shape=jax.ShapeDtypeStruct((bsz, heads, d), BF16),
        compiler_params=_params("parallel", "arbitrary"),
        name="sb_sample_attn",
    )(page_table, q, k_new, v_new, *([cache_k] * pp), *([cache_v] * pp))


def _win_softmax(s, sink):
    m = jnp.maximum(jnp.max(s, axis=-1, keepdims=True), sink)
    p = jnp.exp(s - m)
    denom = jnp.sum(p, axis=-1, keepdims=True) + jnp.exp(sink - m)
    return p / denom


def _win_prompt_kernel(sink_ref, q_ref, kp_ref, kc_ref, vp_ref, vc_ref, o_ref, *, kv_heads, group, d, scale):
    nb = pl.program_id(1)
    tq = q_ref.shape[0]
    t_loc = lax.broadcasted_iota(jnp.int32, (tq, 2 * tq), 0)
    j_loc = lax.broadcasted_iota(jnp.int32, (tq, 2 * tq), 1)
    dist = t_loc + tq - j_loc
    k_pos = (nb - 1) * tq + j_loc
    valid = (dist >= 0) & (dist <= WINDOW) & (k_pos >= 0)
    dist_f = dist.astype(F32)
    for g in range(kv_heads):
        kb = jnp.concatenate([kp_ref[:, g * d:(g + 1) * d], kc_ref[:, g * d:(g + 1) * d]], axis=0).astype(BF16)
        vb = jnp.concatenate([vp_ref[:, g * d:(g + 1) * d], vc_ref[:, g * d:(g + 1) * d]], axis=0).astype(BF16)
        outs = []
        for r in range(group):
            h = g * group + r
            slope = 2.0 ** (-8.0 * (h + 1) / (kv_heads * group))
            qh = q_ref[:, h * d:(h + 1) * d].astype(BF16)
            s = _dot_nt(qh, kb) * scale
            s = jnp.where(valid, s - slope * dist_f, NEG_INF)
            p = _win_softmax(s, sink_ref[h])
            outs.append(_dot(p.astype(BF16), vb))
        for r in range(0, group, 2):
            h = g * group + r
            o_ref[:, h * d:(h + 2) * d] = jnp.concatenate(outs[r:r + 2], axis=-1).astype(o_ref.dtype)


def _win_prompt_attn(qkv, sinks, *, batch, seq, heads, kv_heads, d):
    tq = WINDOW
    nb = seq // tq
    group = heads // kv_heads
    qw = heads * d
    kvw = kv_heads * d
    kcol = qw // kvw
    vcol = kcol + 1
    grid_spec = pltpu.PrefetchScalarGridSpec(
        num_scalar_prefetch=0,
        grid=(batch, nb),
        in_specs=[pl.BlockSpec(memory_space=pltpu.SMEM),
                  pl.BlockSpec((tq, qw), lambda b, i: (b * nb + i, 0)),
                  pl.BlockSpec((tq, kvw), lambda b, i: (b * nb + jnp.maximum(i - 1, 0), kcol)),
                  pl.BlockSpec((tq, kvw), lambda b, i: (b * nb + i, kcol)),
                  pl.BlockSpec((tq, kvw), lambda b, i: (b * nb + jnp.maximum(i - 1, 0), vcol)),
                  pl.BlockSpec((tq, kvw), lambda b, i: (b * nb + i, vcol))],
        out_specs=pl.BlockSpec((tq, qw), lambda b, i: (b * nb + i, 0)),
    )
    return pl.pallas_call(
        functools.partial(_win_prompt_kernel, kv_heads=kv_heads, group=group, d=d, scale=d ** -0.5),
        grid_spec=grid_spec,
        out_shape=jax.ShapeDtypeStruct((batch * seq, qw), BF16),
        compiler_params=_params("parallel", "arbitrary"),
        name="win_prompt_attn",
    )(sinks, qkv, qkv, qkv, qkv, qkv)


def _win_sample_kernel(sink_ref, q_ref, kn_ref, vn_ref, kb_ref, vb_ref, o_ref, ko_ref, vo_ref,
                       *, group, d, scale, past_len):
    heads = q_ref.shape[0]
    n_buf, width = kb_ref.shape
    qbd = _block_diag_queries(q_ref[...], group, d, width)
    kbuf = kb_ref[...]
    vbuf = vb_ref[...]
    kn = kn_ref[...]
    vn = vn_ref[...]
    slope = jnp.exp2(-8.0 * (lax.broadcasted_iota(jnp.int32, (heads, 1), 0) + 1).astype(F32) / heads)
    q_pos = past_len
    j = lax.broadcasted_iota(jnp.int32, (heads, n_buf), 1)
    k_pos = past_len - n_buf + j
    dist = q_pos - k_pos
    valid = (dist >= 0) & (dist <= WINDOW) & (k_pos >= 0)
    s_buf = _dot_nt(qbd.astype(BF16), kbuf.astype(BF16)) * scale
    s_buf = jnp.where(valid, s_buf - slope * dist.astype(F32), NEG_INF)
    s_new = jnp.sum(qbd * kn, axis=-1, keepdims=True) * scale
    sink = sink_ref[...]
    m = jnp.maximum(jnp.maximum(jnp.max(s_buf, axis=-1, keepdims=True), s_new), sink)
    p_buf = jnp.exp(s_buf - m)
    p_new = jnp.exp(s_new - m)
    denom = jnp.sum(p_buf, axis=-1, keepdims=True) + p_new + jnp.exp(sink - m)
    acc = _dot((p_buf / denom).astype(BF16), vbuf.astype(BF16)) + (p_new / denom) * vn
    o_ref[...] = _select_own_block(acc, group, d).astype(o_ref.dtype)
    last = lax.broadcasted_iota(jnp.int32, (n_buf, width), 0) == n_buf - 1
    ko_ref[...] = jnp.where(last, kn, pltpu.roll(kbuf, n_buf - 1, 0))
    vo_ref[...] = jnp.where(last, vn, pltpu.roll(vbuf, n_buf - 1, 0))


def _win_sample_attn(q, k_new, v_new, buf_k, buf_v, sinks, *, group, d, past_len):
    bsz, heads, _ = q.shape
    _, n_buf, width = buf_k.shape
    row = lambda b: (b, 0, 0)
    return pl.pallas_call(
        functools.partial(_win_sample_kernel, group=group, d=d, scale=d ** -0.5, past_len=past_len),
        grid=(bsz,),
        in_specs=[pl.BlockSpec((heads, 1), lambda b: (0, 0)),
                  pl.BlockSpec((None, heads, d), row),
                  pl.BlockSpec((None, 1, width), row),
                  pl.BlockSpec((None, 1, width), row),
                  pl.BlockSpec((None, n_buf, width), row),
                  pl.BlockSpec((None, n_buf, width), row)],
        out_specs=[pl.BlockSpec((None, heads, d), row),
                   pl.BlockSpec((None, n_buf, width), row),
                   pl.BlockSpec((None, n_buf, width), row)],
        out_shape=[jax.ShapeDtypeStruct((bsz, heads, d), BF16),
                   jax.ShapeDtypeStruct((bsz, n_buf, width), F32),
                   jax.ShapeDtypeStruct((bsz, n_buf, width), F32)],
        compiler_params=_params("parallel"),
        name="win_sample_attn",
    )(sinks.reshape(heads, 1), q, k_new, v_new, buf_k, buf_v)


def _rope_tables(pos, rope_dim):
    inv_freq = ROPE_THETA ** (-jnp.arange(0, rope_dim, 2, dtype=F32) / rope_dim)
    ang = pos.astype(F32)[:, None] * inv_freq[None, :]
    cos, sin = jnp.cos(ang), jnp.sin(ang)
    reps = LANES // rope_dim
    return (jnp.concatenate([cos, cos] * reps, axis=-1),
            jnp.concatenate([-sin, sin] * reps, axis=-1))


def _rope_lanes(x, cos_t, sin_t, half):
    lane = lax.broadcasted_iota(jnp.int32, x.shape, 1)
    partner = jnp.where(lax.rem(lane, 2 * half) < half,
                        pltpu.roll(x, LANES - half, 1), pltpu.roll(x, half, 1))
    return x * cos_t + partner * sin_t


def _mla_down_kernel(x_ref, w_ref, qn_ref, kvn_ref, cos_ref, sin_ref, cq_ref, ckv_ref, kr_ref, kr2_ref,
                     *, q_lora, kv_lora, rope_dim):
    acc = _dot(x_ref[...].astype(BF16), w_ref[...])
    cq_ref[...] = _rms_norm(acc[:, :q_lora], qn_ref[...]).astype(cq_ref.dtype)
    ckv_ref[...] = _rms_norm(acc[:, q_lora:q_lora + kv_lora], kvn_ref[...])
    kr = acc[:, q_lora + kv_lora:]
    kr = _rope_lanes(kr, cos_ref[...], sin_ref[...], rope_dim // 2)
    lane = lax.broadcasted_iota(jnp.int32, kr.shape, 1)
    kr = jnp.where(lane < rope_dim, kr, 0.0)
    kr_ref[...] = kr[:, :rope_dim]
    kr2_ref[...] = (kr + pltpu.roll(kr, rope_dim, 1)).astype(kr2_ref.dtype)


def _mla_down(x, w_pad, q_norm, kv_norm, cos_t, sin_t, *, q_lora, kv_lora, rope_dim, tm, name):
    m, k = x.shape
    n = w_pad.shape[1]
    rowb = lambda w: pl.BlockSpec((tm, w), lambda i: (i, 0))
    full = lambda r, c: pl.BlockSpec((r, c), lambda i: (0, 0))
    return pl.pallas_call(
        functools.partial(_mla_down_kernel, q_lora=q_lora, kv_lora=kv_lora, rope_dim=rope_dim),
        grid=(m // tm,),
        in_specs=[rowb(k), full(k, n), full(1, q_lora), full(1, kv_lora), rowb(LANES), rowb(LANES)],
        out_specs=[rowb(q_lora), rowb(kv_lora), rowb(rope_dim), rowb(LANES)],
        out_shape=[jax.ShapeDtypeStruct((m, q_lora), BF16),
                   jax.ShapeDtypeStruct((m, kv_lora), F32),
                   jax.ShapeDtypeStruct((m, rope_dim), F32),
                   jax.ShapeDtypeStruct((m, LANES), BF16)],
        compiler_params=_params("parallel"),
        name=name,
    )(x, w_pad, q_norm.reshape(1, -1), kv_norm.reshape(1, -1), cos_t, sin_t)


def _mla_uq_kernel(cq_ref, w_ref, cos_ref, sin_ref, qn_ref, qr_ref, *, nope_w, rope_dim):
    acc = _dot(cq_ref[...], w_ref[...])
    qn_ref[...] = acc[:, :nope_w].astype(qn_ref.dtype)
    cos_t = cos_ref[...]
    sin_t = sin_ref[...]
    for c in range(qr_ref.shape[1] // LANES):
        x = acc[:, nope_w + c * LANES:nope_w + (c + 1) * LANES]
        qr_ref[:, c * LANES:(c + 1) * LANES] = _rope_lanes(x, cos_t, sin_t, rope_dim // 2).astype(qr_ref.dtype)


def _mla_uq(cq, w_perm, cos_t, sin_t, *, nope_w, rope_w, rope_dim, tm, name):
    m, k = cq.shape
    n = w_perm.shape[1]
    rowb = lambda w: pl.BlockSpec((tm, w), lambda i: (i, 0))
    return pl.pallas_call(
        functools.partial(_mla_uq_kernel, nope_w=nope_w, rope_dim=rope_dim),
        grid=(m // tm,),
        in_specs=[rowb(k), pl.BlockSpec((k, n), lambda i: (0, 0)), rowb(LANES), rowb(LANES)],
        out_specs=[rowb(nope_w), rowb(rope_w)],
        out_shape=[jax.ShapeDtypeStruct((m, nope_w), BF16), jax.ShapeDtypeStruct((m, rope_w), BF16)],
        compiler_params=_params("parallel"),
        name=name,
    )(cq, w_perm, cos_t, sin_t)


def _mla_prompt_kernel(qn_ref, qr_ref, kn_ref, kr_ref, v_ref, o_ref, m_scr, l_scr, acc_scr,
                       *, tq, d, rope_dim, scale):
    qi = pl.program_id(2)
    lane = lax.broadcasted_iota(jnp.int32, (tq, LANES), 1)
    t_loc = lax.broadcasted_iota(jnp.int32, (tq, tq), 0)
    s_loc = lax.broadcasted_iota(jnp.int32, (tq, tq), 1)
    causal = s_loc <= t_loc
    qr_all = qr_ref[...]
    n_h = LANES // rope_dim
    qns = [qn_ref[:, hh * d:(hh + 1) * d] for hh in range(n_h)]
    qrs = [jnp.where((lane >= hh * rope_dim) & (lane < (hh + 1) * rope_dim), qr_all, 0.0).astype(BF16)
           for hh in range(n_h)]

    def run(kb_first, nblk, mask):
        start = pl.multiple_of(kb_first * tq, tq)
        keys = pl.ds(start, nblk * tq)
        kr = kr_ref[keys, :]
        for hh in range(n_h):
            cols = slice(hh * d, (hh + 1) * d)
            s = (_dot_nt(qns[hh], kn_ref[keys, cols]) + _dot_nt(qrs[hh], kr)) * scale
            if mask is not None:
                s = jnp.where(mask, s, NEG_INF)
            m = m_scr[hh]
            m_new = jnp.maximum(m, jnp.max(s, axis=-1, keepdims=True))
            alpha = jnp.exp(m - m_new)
            p = jnp.exp(s - m_new)
            m_scr[hh] = m_new
            l_scr[hh] = alpha * l_scr[hh] + jnp.sum(p, axis=-1, keepdims=True)
            acc_scr[hh] = alpha * acc_scr[hh] + _dot(p.astype(BF16), v_ref[keys, cols])

    m_scr[...] = jnp.full(m_scr.shape, NEG_INF, F32)
    l_scr[...] = jnp.zeros_like(l_scr)
    acc_scr[...] = jnp.zeros_like(acc_scr)
    run(qi, 1, causal)

    one = qi & 1
    two = qi & 2

    @pl.when(one == 1)
    def _():
        run(0, 1, None)

    @pl.when(two == 2)
    def _():
        run(one, 2, None)

    def body(i, carry):
        run(one + two + 4 * i, 4, None)
        return carry

    lax.fori_loop(0, qi >> 2, body, 0)
    for hh in range(n_h):
        o_ref[:, hh * d:(hh + 1) * d] = (acc_scr[hh] / l_scr[hh]).astype(o_ref.dtype)


def _mla_prompt_attn(q_nope, q_rope, kv, kr2, *, batch, seq, heads, d, rope_dim, tq=256):
    nq = seq // tq
    hp = LANES // rope_dim
    return pl.pallas_call(
        functools.partial(_mla_prompt_kernel, tq=tq, d=d, rope_dim=rope_dim, scale=(d + rope_dim) ** -0.5),
        grid=(batch, heads // hp, nq),
        in_specs=[pl.BlockSpec((tq, hp * d), lambda b, h, i: (b * nq + i, h)),
                  pl.BlockSpec((tq, LANES), lambda b, h, i: (b * nq + i, h)),
                  pl.BlockSpec((seq, hp * d), lambda b, h, i: (b, h)),
                  pl.BlockSpec((seq, LANES), lambda b, h, i: (b, 0)),
                  pl.BlockSpec((seq, hp * d), lambda b, h, i: (b, heads // hp + h))],
        out_specs=pl.BlockSpec((tq, hp * d), lambda b, h, i: (b * nq + i, h)),
        out_shape=jax.ShapeDtypeStruct((batch * seq, heads * d), BF16),
        scratch_shapes=[pltpu.VMEM((hp, tq, 1), F32), pltpu.VMEM((hp, tq, 1), F32),
                        pltpu.VMEM((hp, tq, d), F32)],
        compiler_params=_params("parallel", "parallel", "arbitrary"),
        name="mla_prompt_attn",
    )(q_nope, q_rope, kv, kr2, kv)


def _mla_sample_kernel(pt_ref, ql_ref, qr_ref, cn_ref, rn_ref, *rest, pp, scale, n_past):
    c_refs = rest[:pp]
    r_refs = rest[pp:2 * pp]
    o_ref, m_scr, l_scr, acc_scr = rest[2 * pp:]
    s_id = pl.program_id(1)
    heads, width = ql_ref.shape
    ql = ql_ref[...]
    qr = qr_ref[...]

    @pl.when(s_id == 0)
    def _():
        cn = cn_ref[...]
        s = (jnp.sum(ql * cn, axis=-1, keepdims=True) + jnp.sum(qr * rn_ref[...], axis=-1, keepdims=True)) * scale
        q_pos = n_past + lax.broadcasted_iota(jnp.int32, (heads, 1), 1)
        k_pos = n_past + lax.broadcasted_iota(jnp.int32, (heads, 1), 1)
        vis = k_pos <= q_pos
        m = jnp.where(vis, s, NEG_INF)
        p = jnp.where(vis, jnp.exp(s - m), 0.0)
        m_scr[...] = m
        l_scr[...] = p
        acc_scr[...] = p * jnp.broadcast_to(cn, (heads, width))

    ql_bf = ql.astype(BF16)
    qr_bf = qr.astype(BF16)
    m = m_scr[...]
    l = l_scr[...]
    acc = acc_scr[...]
    c_cat = jnp.concatenate([c_refs[i][...].astype(BF16) for i in range(pp)], axis=0)
    rt_cat = jnp.concatenate([r_refs[i][...].astype(BF16) for i in range(pp)], axis=1)
    s = (_dot_nt(ql_bf, c_cat) + _dot(qr_bf, rt_cat)) * scale
    m_new = jnp.maximum(m, jnp.max(s, axis=-1, keepdims=True))
    alpha = jnp.exp(m - m_new)
    p = jnp.exp(s - m_new)
    l = alpha * l + jnp.sum(p, axis=-1, keepdims=True)
    acc = alpha * acc + _dot(p.astype(BF16), c_cat)
    m_scr[...] = m_new
    l_scr[...] = l
    acc_scr[...] = acc

    @pl.when(s_id == pl.num_programs(1) - 1)
    def _():
        o_ref[...] = (acc / l).astype(o_ref.dtype)


def _mla_sample_attn(q_lat, q_rope, c_new, r_new, cache_c, cache_rt, layer, page_table, *, scale, pp):
    bsz, heads, width = q_lat.shape
    rope_dim = q_rope.shape[2]
    page = cache_c.shape[2]
    n_pages = page_table.shape[1]
    steps = n_pages // pp

    def page_spec(rows, cols, i):
        return pl.BlockSpec((None, None, rows, cols), lambda b, s, pt: (layer, pt[b, s * pp + i], 0, 0))

    row = lambda b, s, pt: (b, 0, 0)
    grid_spec = pltpu.PrefetchScalarGridSpec(
        num_scalar_prefetch=1,
        grid=(bsz, steps),
        in_specs=[pl.BlockSpec((None, heads, width), row),
                  pl.BlockSpec((None, heads, rope_dim), row),
                  pl.BlockSpec((None, 1, width), row),
                  pl.BlockSpec((None, 1, rope_dim), row)]
                 + [page_spec(page, width, i) for i in range(pp)]
                 + [page_spec(rope_dim, page, i) for i in range(pp)],
        out_specs=pl.BlockSpec((None, heads, width), row),
        scratch_shapes=[pltpu.VMEM((heads, 1), F32), pltpu.VMEM((heads, 1), F32),
                        pltpu.VMEM((heads, width), F32)],
    )
    return pl.pallas_call(
        functools.partial(_mla_sample_kernel, pp=pp, scale=scale, n_past=n_pages * page),
        grid_spec=grid_spec,
        out_shape=jax.ShapeDtypeStruct((bsz, heads, width), BF16),
        compiler_params=_params("parallel", "arbitrary"),
        name="mla_sample_attn",
    )(page_table, q_lat, q_rope, c_new, r_new, *([cache_c] * pp), *([cache_rt] * pp))


def _top_values(x, k):
    tops = []
    for _ in range(k):
        m = jnp.max(x, axis=0, keepdims=True)
        tops.append(m)
        x = jnp.where(x == m, -jnp.inf, x)
    return jnp.concatenate(tops, axis=0)


def _peer_route_kernel(x_ref, wq_ref, sk_ref, th_ref, sb_ref, ea_ref, eb_ref, *, heads, half):
    q = _dot(x_ref[...].astype(BF16), wq_ref[...]).astype(BF16)
    for h in range(heads):
        qa = q[:, (2 * h) * half:(2 * h + 1) * half]
        qb = q[:, (2 * h + 1) * half:(2 * h + 2) * half]
        sa = _dot_nt(sk_ref[0], qa)
        sb = _dot_nt(sk_ref[1], qb)
        k1 = PEER_TOPK + 1
        ta = _top_values(sa, k1)
        tb = _top_values(sb, k1)
        cand = jnp.concatenate([ta[i:i + 1] + tb[:k1 // (i + 1)] for i in range(k1)], axis=0)
        tc = _top_values(cand, k1)
        tau = 0.5 * (tc[PEER_TOPK - 1:PEER_TOPK] + tc[PEER_TOPK:k1])
        best = ta[0:1] + tb[0:1]
        z = jnp.sum(jnp.where(cand > tau, jnp.exp(cand - best), 0.0), axis=0, keepdims=True)
        th_ref[h] = tau - sa
        ea_ref[h] = jnp.exp(sa - ta[0:1]) / z
        eb = jnp.exp(sb - tb[0:1])
        for tt in range(sb.shape[1] // LANES):
            sb_ref[h, tt] = sb[:, tt * LANES:(tt + 1) * LANES]
            eb_ref[h, tt] = eb[:, tt * LANES:(tt + 1) * LANES]


def _peer_route(x, w_q, subkeys, *, heads, tn, name):
    n, dm = x.shape
    _, keys, half = subkeys.shape
    big = pl.BlockSpec((heads, keys, tn), lambda i: (0, 0, i))
    big_shape = jax.ShapeDtypeStruct((heads, keys, n), F32)
    tiled = pl.BlockSpec((heads, tn // LANES, keys, LANES), lambda i: (0, i, 0, 0))
    tiled_shape = jax.ShapeDtypeStruct((heads, n // LANES, keys, LANES), F32)
    return pl.pallas_call(
        functools.partial(_peer_route_kernel, heads=heads, half=half),
        grid=(n // tn,),
        in_specs=[pl.BlockSpec((tn, dm), lambda i: (i, 0)),
                  pl.BlockSpec(w_q.shape, lambda i: (0, 0)),
                  pl.BlockSpec(subkeys.shape, lambda i: (0, 0, 0))],
        out_specs=[big, tiled, big, tiled],
        out_shape=[big_shape, tiled_shape, big_shape, tiled_shape],
        compiler_params=_params("parallel"),
        name=name,
    )(x, w_q, subkeys)


def _peer_dense_kernel(x_ref, th_ref, sb_ref, ea_ref, eb_ref, u_ref, vt_ref, g_ref, b_ref, o_ref,
                       xt_scr, yt_scr, ht_even, ht_odd, pt_even, pt_odd, *, heads, keys, alpha):
    j = pl.program_id(1)
    tn = x_ref.shape[0]
    ce = u_ref.shape[0]

    @pl.when(j == 0)
    def _():
        xt_scr[...] = x_ref[...].T.astype(BF16)
        yt_scr[...] = jnp.zeros_like(yt_scr)
        ht_odd[...] = jnp.zeros_like(ht_odd)
        pt_even[...] = jnp.zeros_like(pt_even)

    def step(ht_up, ht_gate, pt_gate, pt_down):
        n_a = ce // keys
        yrows = yt_scr.shape[0] // n_a
        xt = xt_scr[...]
        for ai in range(n_a):
            rows = slice(ai * keys, (ai + 1) * keys)
            out_rows = slice(ai * yrows, (ai + 1) * yrows)
            ht_up[rows, :] = _dot(u_ref[rows, :], xt)
            yt_scr[out_rows, :] += _dot(vt_ref[out_rows, :], pt_down[...])
            for tt in range(tn // LANES):
                tok = slice(tt * LANES, (tt + 1) * LANES)
                gate = jnp.zeros((keys, LANES), F32)
                for h in range(heads):
                    chosen = sb_ref[h, tt] > th_ref[h, ai:ai + 1, tok]
                    gate = gate + jnp.where(chosen, ea_ref[h, ai:ai + 1, tok] * eb_ref[h, tt], 0.0)
                hv = ht_gate[ai * keys:(ai + 1) * keys, tok]
                pt_gate[ai * keys:(ai + 1) * keys, tok] = (gate * _gelu(hv)).astype(BF16)

    @pl.when((j & 1) == 0)
    def _():
        step(ht_even, ht_odd, pt_odd, pt_even)

    @pl.when((j & 1) == 1)
    def _():
        step(ht_odd, ht_even, pt_even, pt_odd)

    @pl.when(j == pl.num_programs(1) - 1)
    def _():
        o_ref[...] = _layer_norm(alpha * x_ref[...] + yt_scr[...].T, g_ref[...], b_ref[...])


def _peer_dense(x, route, u_bf, vt_bf, g, b, *, alpha, tn, ce, name):
    th, sb, ea, eb = route
    n, dm = x.shape
    heads, keys, _ = th.shape
    n_tiles = u_bf.shape[0] // ce
    once = pl.Buffered(1)
    tile = lambda j, lag: jnp.clip(j - lag, 0, n_tiles - 1)
    by_b = pl.BlockSpec((heads, tn // LANES, keys, LANES), lambda i, j: (0, i, 0, 0), pipeline_mode=once)
    by_a = pl.BlockSpec((heads, ce // keys, tn), lambda i, j: (0, tile(j, 1), i))
    return pl.pallas_call(
        functools.partial(_peer_dense_kernel, heads=heads, keys=keys, alpha=alpha),
        grid=(n // tn, n_tiles + 2),
        in_specs=[pl.BlockSpec((tn, dm), lambda i, j: (i, 0), pipeline_mode=once),
                  by_a, by_b, by_a, by_b,
                  pl.BlockSpec((ce, dm), lambda i, j: (tile(j, 0), 0)),
                  pl.BlockSpec((dm, ce), lambda i, j: (0, tile(j, 2))),
                  pl.BlockSpec((1, dm), lambda i, j: (0, 0)),
                  pl.BlockSpec((1, dm), lambda i, j: (0, 0))],
        out_specs=pl.BlockSpec((tn, dm), lambda i, j: (i, 0)),
        out_shape=jax.ShapeDtypeStruct((n, dm), F32),
        scratch_shapes=[pltpu.VMEM((dm, tn), BF16), pltpu.VMEM((dm, tn), F32),
                        pltpu.VMEM((ce, tn), F32), pltpu.VMEM((ce, tn), F32),
                        pltpu.VMEM((ce, tn), BF16), pltpu.VMEM((ce, tn), BF16)],
        compiler_params=_params("parallel", "arbitrary"),
        name=name,
    )(x, th, sb, ea, eb, u_bf, vt_bf, g.reshape(1, dm), b.reshape(1, dm))


def _peer_layer(x, w_q, subkeys, u_bf, vt_bf, g, b, *, alpha, heads, tn_route, tn_dense, ce, tag):
    route = _peer_route(x, w_q, subkeys, heads=heads, tn=tn_route, name="peer_route_" + tag)
    return _peer_dense(x, route, u_bf, vt_bf, g, b, alpha=alpha, tn=tn_dense, ce=ce, name="peer_dense_" + tag)


def kernel(x_prompt, x_sample, cache_sb_k, cache_sb_v, cache_win_k, cache_win_v, cache_mla_ckv, cache_mla_krope, page_table, sb_w_qkv, sb_w_o, win_w_qkv, win_w_o, win_sinks, mla_w_down, mla_q_norm, mla_kv_norm, mla_w_uq, mla_w_uk, mla_w_uv, mla_w_o, peer_w_q, peer_subkeys, peer_u, peer_v, ln_mix_g, ln_mix_b, ln_ffn_g, ln_ffn_b):
    batch, seq, dm = x_prompt.shape
    bsz, dec_seq, _ = x_sample.shape
    assert dec_seq == 1
    depth = peer_w_q.shape[0]
    alpha = (2.0 * depth) ** 0.25
    page = cache_sb_k.shape[2]
    n_pages = page_table.shape[1]
    past_len = n_pages * page

    sb_kv, sb_d = cache_sb_k.shape[3], cache_sb_k.shape[4]
    sb_heads = dm // sb_d
    win_kv, win_d = cache_win_k.shape[3], cache_win_k.shape[4]
    win_heads = win_sinks.shape[1]
    q_lora = mla_q_norm.shape[1]
    kv_lora = mla_kv_norm.shape[1]
    rope_dim = cache_mla_krope.shape[3]
    mla_heads, mla_nope = mla_w_uk.shape[2], mla_w_uk.shape[3]
    mla_v = mla_w_uv.shape[3]
    peer_heads = peer_w_q.shape[2] // (2 * peer_subkeys.shape[3])

    n_p = batch * seq
    tm_p = 512 if n_p % 512 == 0 else n_p
    tm_ln = 256 if n_p % 256 == 0 else n_p
    tn_route_p = 256 if n_p % 256 == 0 else n_p
    tn_dense_p = 512 if n_p % 512 == 0 else n_p

    xp = x_prompt.reshape(n_p, dm)
    xs = x_sample.reshape(bsz, dm)

    cache_sb_k2 = cache_sb_k.reshape(cache_sb_k.shape[:2] + (page * sb_kv, sb_d))
    cache_sb_v2 = cache_sb_v.reshape(cache_sb_v.shape[:2] + (page * sb_kv, sb_d))
    pages_per_step = 8 if n_pages % 8 == 0 else 1

    cos_p, sin_p = _rope_tables(jnp.arange(seq), rope_dim)
    cos_p = jnp.tile(cos_p, (batch, 1))
    sin_p = jnp.tile(sin_p, (batch, 1))
    cos_s, sin_s = _rope_tables(jnp.full((bsz,), past_len), rope_dim)

    outs = {k: [] for k in ("sb_kp", "sb_vp", "sb_ks", "sb_vs", "win_kp", "win_vp", "win_ks", "win_vs",
                            "mla_cp", "mla_rp", "mla_cs", "mla_rs")}

    for i in range(depth):
        li, kind = divmod(i, 3)
        g_mix, b_mix = ln_mix_g[i], ln_mix_b[i]
        if kind == 0:
            w_qkv = sb_w_qkv[li].astype(BF16)
            w_o = sb_w_o[li].astype(BF16)
            qw, kvw = sb_heads * sb_d, sb_kv * sb_d
            qkv_p = _matmul(xp, w_qkv, tm=tm_p, name="sb_qkv_p")
            qkv_s = _matmul(xs, w_qkv, tm=bsz, name="sb_qkv_s")
            o_p = _sb_prompt_attn(qkv_p, batch=batch, seq=seq, heads=sb_heads, kv_heads=sb_kv, d=sb_d)
            k_s = qkv_s[:, qw:qw + kvw]
            v_s = qkv_s[:, qw + kvw:]
            o_s = _sb_sample_attn(qkv_s[:, :qw].reshape(bsz, sb_heads, sb_d), k_s.reshape(bsz, 1, kvw),
                                  v_s.reshape(bsz, 1, kvw), cache_sb_k2, cache_sb_v2, li, page_table,
                                  group=sb_heads // sb_kv, d=sb_d, page=page, pp=pages_per_step)
            xp = _matmul_ln(o_p, w_o, xp, g_mix, b_mix, alpha=alpha, tm=tm_ln, name="sb_out_p")
            xs = _matmul_ln(o_s.reshape(bsz, qw), w_o, xs, g_mix, b_mix, alpha=alpha, tm=bsz, name="sb_out_s")
            outs["sb_kp"].append(qkv_p[:, qw:qw + kvw].reshape(batch, seq, sb_kv, sb_d))
            outs["sb_vp"].append(qkv_p[:, qw + kvw:].reshape(batch, seq, sb_kv, sb_d))
            outs["sb_ks"].append(k_s.reshape(bsz, 1, sb_kv, sb_d))
            outs["sb_vs"].append(v_s.reshape(bsz, 1, sb_kv, sb_d))
        elif kind == 1:
            w_qkv = win_w_qkv[li].astype(BF16)
            w_o = win_w_o[li].astype(BF16)
            qw, kvw = win_heads * win_d, win_kv * win_d
            qkv_p = _matmul(xp, w_qkv, tm=tm_p, name="win_qkv_p")
            qkv_s = _matmul(xs, w_qkv, tm=bsz, name="win_qkv_s")
            o_p = _win_prompt_attn(qkv_p, win_sinks[li], batch=batch, seq=seq, heads=win_heads,
                                   kv_heads=win_kv, d=win_d)
            n_buf = cache_win_k.shape[2]
            o_s, nk_s, nv_s = _win_sample_attn(
                qkv_s[:, :qw].reshape(bsz, win_heads, win_d), qkv_s[:, qw:qw + kvw].reshape(bsz, 1, kvw),
                qkv_s[:, qw + kvw:].reshape(bsz, 1, kvw), cache_win_k[li].reshape(bsz, n_buf, kvw),
                cache_win_v[li].reshape(bsz, n_buf, kvw), win_sinks[li],
                group=win_heads // win_kv, d=win_d, past_len=past_len)
            xp = _matmul_ln(o_p, w_o, xp, g_mix, b_mix, alpha=alpha, tm=tm_ln, name="win_out_p")
            xs = _matmul_ln(o_s.reshape(bsz, qw), w_o, xs, g_mix, b_mix, alpha=alpha, tm=bsz, name="win_out_s")
            keep = min(WINDOW, seq)
            kp = qkv_p[:, qw:qw + kvw].reshape(batch, seq, win_kv, win_d)
            vp = qkv_p[:, qw + kvw:].reshape(batch, seq, win_kv, win_d)
            outs["win_kp"].append(kp[:, seq - keep:])
            outs["win_vp"].append(vp[:, seq - keep:])
            outs["win_ks"].append(nk_s.reshape(bsz, n_buf, win_kv, win_d))
            outs["win_vs"].append(nv_s.reshape(bsz, n_buf, win_kv, win_d))
        else:
            down_w = q_lora + kv_lora + rope_dim
            w_down = jnp.pad(mla_w_down[li], ((0, 0), (0, q_lora + kv_lora + LANES - down_w))).astype(BF16)
            w_uq = mla_w_uq[li].reshape(q_lora, mla_heads, mla_nope + rope_dim)
            w_uq = jnp.concatenate([w_uq[:, :, :mla_nope].reshape(q_lora, -1),
                                    w_uq[:, :, mla_nope:].reshape(q_lora, -1)], axis=1).astype(BF16)
            nope_w, rope_w = mla_heads * mla_nope, mla_heads * rope_dim
            w_kv = jnp.concatenate([mla_w_uk[li].reshape(kv_lora, -1), mla_w_uv[li].reshape(kv_lora, -1)],
                                   axis=1).astype(BF16)
            w_uk_t = jnp.transpose(mla_w_uk[li], (1, 2, 0)).astype(BF16)
            w_uv_h = jnp.transpose(mla_w_uv[li], (1, 0, 2)).astype(BF16)
            w_o = mla_w_o[li].astype(BF16)
            mla_kw = dict(q_lora=q_lora, kv_lora=kv_lora, rope_dim=rope_dim)
            uq_kw = dict(nope_w=nope_w, rope_w=rope_w, rope_dim=rope_dim)
            scale = (mla_nope + rope_dim) ** -0.5

            cq_p, ckv_p, kr_p, kr2_p = _mla_down(xp, w_down, mla_q_norm[li], mla_kv_norm[li], cos_p, sin_p,
                                                 tm=tm_p, name="mla_down_p", **mla_kw)
            qn_p, qr_p = _mla_uq(cq_p, w_uq, cos_p, sin_p, tm=tm_p, name="mla_uq_p", **uq_kw)
            kv_p = _matmul(ckv_p, w_kv, tm=tm_p, out_dtype=BF16, name="mla_kv_p")
            o_p = _mla_prompt_attn(qn_p, qr_p, kv_p, kr2_p, batch=batch, seq=seq, heads=mla_heads,
                                   d=mla_nope, rope_dim=rope_dim, tq=256 if seq % 256 == 0 else seq)
            xp = _matmul_ln(o_p, w_o, xp, g_mix, b_mix, alpha=alpha, tm=tm_ln, name="mla_out_p")

            cq_s, ckv_s, kr_s, _ = _mla_down(xs, w_down, mla_q_norm[li], mla_kv_norm[li], cos_s, sin_s,
                                             tm=bsz, name="mla_down_s", **mla_kw)
            qn_s, qr_s = _mla_uq(cq_s, w_uq, cos_s, sin_s, tm=bsz, name="mla_uq_s", **uq_kw)
            q_lat = _head_matmul(qn_s, w_uk_t, out_dtype=F32, name="mla_qlat_s")
            o_lat = _mla_sample_attn(q_lat.reshape(bsz, mla_heads, kv_lora),
                                     qr_s.astype(F32).reshape(bsz, mla_heads, rope_dim),
                                     ckv_s.reshape(bsz, 1, kv_lora), kr_s.reshape(bsz, 1, rope_dim),
                                     cache_mla_ckv, jnp.swapaxes(cache_mla_krope, 2, 3), li, page_table,
                                     scale=scale, pp=pages_per_step)
            o_s = _head_matmul(o_lat.reshape(bsz, mla_heads * kv_lora), w_uv_h, out_dtype=BF16, name="mla_ov_s")
            xs = _matmul_ln(o_s, w_o, xs, g_mix, b_mix, alpha=alpha, tm=bsz, name="mla_out_s")
            outs["mla_cp"].append(ckv_p.reshape(batch, seq, kv_lora))
            outs["mla_rp"].append(kr_p.reshape(batch, seq, rope_dim))
            outs["mla_cs"].append(ckv_s.reshape(bsz, 1, kv_lora))
            outs["mla_rs"].append(kr_s.reshape(bsz, 1, rope_dim))

        w_q = peer_w_q[i].astype(BF16)
        subkeys = peer_subkeys[i].astype(BF16)
        u_bf = peer_u[i].astype(BF16)
        vt_bf = peer_v[i].astype(BF16).T
        peer_kw = dict(alpha=alpha, heads=peer_heads, ce=8 * peer_subkeys.shape[2])
        xp = _peer_layer(xp, w_q, subkeys, u_bf, vt_bf, ln_ffn_g[i], ln_ffn_b[i], tn_route=tn_route_p,
                         tn_dense=tn_dense_p, tag="p", **peer_kw)
        xs = _peer_layer(xs, w_q, subkeys, u_bf, vt_bf, ln_ffn_g[i], ln_ffn_b[i], tn_route=bsz,
                         tn_dense=bsz, tag="s", **peer_kw)

    stack = lambda k: jnp.stack(outs[k])
    return (xp.reshape(batch, seq, dm), xs.reshape(bsz, 1, dm),
            stack("sb_kp"), stack("sb_vp"), stack("sb_ks"), stack("sb_vs"),
            stack("win_kp"), stack("win_vp"), stack("win_ks"), stack("win_vs"),
            stack("mla_cp"), stack("mla_rp"), stack("mla_cs"), stack("mla_rs"))
```
